```python
import math
import jax, jax.numpy as jnp
from jax import lax
import numpy as np

D_MODEL = 1024
BATCH = 16
SEQ = 2048
DEPTH = 2

CHUNK = 64
Q_BLOCK = 128
D_RNN = 1024
N_RNN_BLOCKS = 8
RNN_BLOCK = D_RNN // N_RNN_BLOCKS
CONV_WIDTH = 4
LRU_C = 8.0
N_HEADS = 16
HEAD_DIM = 64
D_ATTN = N_HEADS * HEAD_DIM
D_FF = 2816
EPS = 1e-6
N_BRANCH = 2
OFF_LRU_X = 0
OFF_LRU_G = OFF_LRU_X + D_RNN
OFF_Q = OFF_LRU_G + D_RNN
OFF_K = OFF_Q + D_ATTN
OFF_V = OFF_K + D_ATTN
OFF_F = OFF_V + D_ATTN
OFF_GATE = OFF_F + N_HEADS
D_IN = OFF_GATE + N_BRANCH * D_MODEL

kernel_name = "hybrid_rglru_fox_macaron_encoder"


def rms_norm(x, g):
    xf = x.astype(jnp.float32)
    y = xf * lax.rsqrt(jnp.mean(xf * xf, axis=-1, keepdims=True) + EPS)
    return (y * g.astype(jnp.float32)).astype(x.dtype)


def swiglu_ffn(h, w_up, w_down):
    gate, up = jnp.split(h @ w_up, 2, axis=-1)
    return (jax.nn.silu(gate) * up) @ w_down


def causal_depthwise_conv(x, w, b):
    y = lax.conv_general_dilated(
        x, w[:, None, :].astype(x.dtype), window_strides=(1,),
        padding=[(CONV_WIDTH - 1, 0)],
        dimension_numbers=("NWC", "WIO", "NWC"),
        feature_group_count=x.shape[-1])
    return y + b


def rg_lru(x, w_a, b_a, w_x, b_x, lam):
    bsz, seq, _ = x.shape
    xb = x.reshape(bsz, seq, N_RNN_BLOCKS, RNN_BLOCK)
    r = jax.nn.sigmoid((jnp.einsum("bsnc,ncd->bsnd", xb, w_a).reshape(bsz, seq, D_RNN) + b_a).astype(jnp.float32))
    i = jax.nn.sigmoid((jnp.einsum("bsnc,ncd->bsnd", xb, w_x).reshape(bsz, seq, D_RNN) + b_x).astype(jnp.float32))
    log_a = -LRU_C * r * jax.nn.softplus(-lam.astype(jnp.float32))
    a = jnp.exp(log_a)
    u = jnp.sqrt(-jnp.expm1(2.0 * log_a)) * (i * x.astype(jnp.float32))

    def combine(left, right):
        a_l, b_l = left
        a_r, b_r = right
        return a_l * a_r, a_r * b_l + b_r

    _, h = lax.associative_scan(combine, (a, u), axis=1)
    return h.astype(x.dtype)


def forgetting_attention(q, k, v, f_logit, g_q, g_k):
    bsz, seq = q.shape[0], q.shape[1]
    qn = rms_norm(q, g_q).transpose(0, 2, 1, 3)
    kn = rms_norm(k, g_k).transpose(0, 2, 1, 3)
    vh = v.transpose(0, 2, 1, 3)
    cum = jnp.cumsum(jax.nn.log_sigmoid(f_logit.astype(jnp.float32)), axis=1).transpose(0, 2, 1)
    scale = HEAD_DIM ** -0.5
    outs = []
    for blk in range(seq // Q_BLOCK):
        q0 = blk * Q_BLOCK
        q1 = q0 + Q_BLOCK
        s = jnp.einsum("bhqd,bhkd->bhqk", qn[:, :, q0:q1], kn[:, :, :q1]).astype(jnp.float32) * scale
        s = s + cum[:, :, q0:q1, None] - cum[:, :, None, :q1]
        mask = jnp.arange(q0, q1)[:, None] >= jnp.arange(q1)[None, :]
        s = jnp.where(mask, s, -jnp.inf)
        p = jax.nn.softmax(s, axis=-1)
        outs.append(jnp.einsum("bhqk,bhkd->bhqd", p.astype(vh.dtype), vh[:, :, :q1]))
    o = jnp.concatenate(outs, axis=2)
    return o.transpose(0, 2, 1, 3).reshape(bsz, seq, D_ATTN)


def setup_inputs(seed: int = 0) -> dict:
    key = jax.random.key(seed)
    ks = jax.random.split(key, 24)
    L, D, F = DEPTH, D_MODEL, D_FF
    nrm = lambda k, shape, fan_in: jax.random.normal(k, shape, jnp.float32) * (fan_in ** -0.5)
    gain = lambda k, shape: 1.0 + 0.02 * jax.random.normal(k, shape, jnp.float32)
    small = lambda k, shape: 0.01 * jax.random.normal(k, shape, jnp.float32)
    u = jax.random.uniform(ks[13], (L, D_RNN), jnp.float32, 0.9, 0.999)
    a0 = u ** (1.0 / LRU_C)
    lam = jnp.log(a0) - jnp.log1p(-a0)
    return {
        "x": jax.random.normal(ks[0], (BATCH, SEQ, D), jnp.float32),
        "g_ffn1": gain(ks[1], (L, D)),
        "w_up1": nrm(ks[2], (L, D, 2 * F), D),
        "w_down1": nrm(ks[3], (L, F, D), F),
        "g_mix": gain(ks[4], (L, D)),
        "w_in": nrm(ks[5], (L, D, D_IN), D),
        "b_gate": small(ks[6], (L, N_BRANCH * D)),
        "conv_w": nrm(ks[7], (L, CONV_WIDTH, D_RNN), CONV_WIDTH),
        "conv_b": small(ks[8], (L, D_RNN)),
        "w_a": nrm(ks[9], (L, N_RNN_BLOCKS, RNN_BLOCK, RNN_BLOCK), RNN_BLOCK),
        "b_a": small(ks[10], (L, D_RNN)),
        "w_x": nrm(ks[11], (L, N_RNN_BLOCKS, RNN_BLOCK, RNN_BLOCK), RNN_BLOCK),
        "b_x": small(ks[12], (L, D_RNN)),
        "lam": lam,
        "g_q": gain(ks[14], (L, HEAD_DIM)),
        "g_k": gain(ks[15], (L, HEAD_DIM)),
        "b_forget": jax.random.uniform(ks[16], (L, N_HEADS), jnp.float32, 1.0, 6.0),
        "w_lru_out": nrm(ks[17], (L, D_RNN, D), D_RNN),
        "w_attn_out": nrm(ks[18], (L, D_ATTN, D), D_ATTN),
        "w_o": nrm(ks[19], (L, D, D), D),
        "g_ffn2": gain(ks[20], (L, D)),
        "w_up2": nrm(ks[21], (L, D, 2 * F), D),
        "w_down2": nrm(ks[22], (L, F, D), F),
    }


def reference(x, g_ffn1, w_up1, w_down1, g_mix, w_in, b_gate, conv_w, conv_b,
              w_a, b_a, w_x, b_x, lam, g_q, g_k, b_forget, w_lru_out, w_attn_out,
              w_o, g_ffn2, w_up2, w_down2):
    bsz, seq, _ = x.shape
    assert seq % CHUNK == 0 and seq % Q_BLOCK == 0
    for l in range(DEPTH):
        x = x + 0.5 * swiglu_ffn(rms_norm(x, g_ffn1[l]), w_up1[l], w_down1[l])

        h = rms_norm(x, g_mix[l])
        z = h @ w_in[l]
        z_lx = z[..., OFF_LRU_X:OFF_LRU_G]
        z_lg = z[..., OFF_LRU_G:OFF_Q]
        q = z[..., OFF_Q:OFF_K].reshape(bsz, seq, N_HEADS, HEAD_DIM)
        k = z[..., OFF_K:OFF_V].reshape(bsz, seq, N_HEADS, HEAD_DIM)
        v = z[..., OFF_V:OFF_F].reshape(bsz, seq, N_HEADS, HEAD_DIM)
        f_logit = z[..., OFF_F:OFF_GATE] + b_forget[l]
        gates = jax.nn.sigmoid(z[..., OFF_GATE:] + b_gate[l])
        gate_lru, gate_attn = jnp.split(gates, N_BRANCH, axis=-1)

        xr = causal_depthwise_conv(z_lx, conv_w[l], conv_b[l])
        hr = rg_lru(xr, w_a[l], b_a[l], w_x[l], b_x[l], lam[l])
        y_lru = (jax.nn.gelu(z_lg) * hr) @ w_lru_out[l]

        o = forgetting_attention(q, k, v, f_logit, g_q[l], g_k[l])
        y_attn = o @ w_attn_out[l]

        m = gate_lru * y_lru + gate_attn * y_attn
        x = x + m @ w_o[l]

        x = x + 0.5 * swiglu_ffn(rms_norm(x, g_ffn2[l]), w_up2[l], w_down2[l])
    return x
```

```python
import functools
import math

import jax
import jax.numpy as jnp
from jax import lax
from jax.experimental import pallas as pl
from jax.experimental.pallas import tpu as pltpu

D_MODEL = 1024
D_RNN = 1024
N_RNN_BLOCKS = 8
RNN_BLOCK = D_RNN // N_RNN_BLOCKS
CONV_WIDTH = 4
LRU_C = 8.0
N_HEADS = 16
HEAD_DIM = 64
D_ATTN = N_HEADS * HEAD_DIM
D_FF = 2816
EPS = 1e-6
OFF_F = 2 * D_RNN + 3 * D_ATTN
OFF_GATE = OFF_F + N_HEADS

LANES = 128
SUBLANES = 8
MXU_DIM = 256
VMEM_LIMIT = 56 * 1024 * 1024

FFN_TM = 256
FFN_FC = 256
MIX_TM = 256
ATT_T = 256
HEADS_PER_STEP = LANES // HEAD_DIM


def _dot(a, b):
    return jnp.dot(a, b, preferred_element_type=jnp.float32)


def _rms_norm(x, g):
    ms = jnp.mean(x * x, axis=-1, keepdims=True)
    return x * lax.rsqrt(ms + EPS) * g


def _const_spec(shape):
    nd = len(shape)
    return pl.BlockSpec(shape, lambda *_: (0,) * nd, pipeline_mode=pl.Buffered(1))


def _swiglu(h, wg_ref, wu_ref, wd_ref):
    acc = jnp.zeros((h.shape[0], D_MODEL), jnp.float32)
    for c in range(D_FF // FFN_FC):
        sl = slice(c * FFN_FC, (c + 1) * FFN_FC)
        gate = _dot(h, wg_ref[:, sl])
        up = _dot(h, wu_ref[:, sl])
        act = (gate * jax.nn.sigmoid(gate) * up).astype(jnp.bfloat16)
        acc = acc + _dot(act, wd_ref[sl, :])
    return acc


def _ffn_kernel(x_ref, g_ref, wg_ref, wu_ref, wd_ref, o_ref):
    x = x_ref[...]
    h = _rms_norm(x, g_ref[...]).astype(jnp.bfloat16)
    o_ref[...] = x + 0.5 * _swiglu(h, wg_ref, wu_ref, wd_ref)


def _ffn(x2d, g, wg, wu, wd):
    n = x2d.shape[0]
    row = pl.BlockSpec((FFN_TM, D_MODEL), lambda i: (i, 0))
    return pl.pallas_call(
        _ffn_kernel,
        grid=(n // FFN_TM,),
        in_specs=[row, _const_spec(g.shape), _const_spec(wg.shape),
                  _const_spec(wu.shape), _const_spec(wd.shape)],
        out_specs=row,
        out_shape=jax.ShapeDtypeStruct(x2d.shape, jnp.float32),
        compiler_params=pltpu.CompilerParams(
            dimension_semantics=("parallel",), vmem_limit_bytes=VMEM_LIMIT),
        name="ffn",
    )(x2d, g, wg, wu, wd)


def _shift_rows(x, d, fill):
    rows = lax.broadcasted_iota(jnp.int32, x.shape, 0)
    return jnp.where(rows >= d, pltpu.roll(x, d, axis=0), fill)


def _group_mean_sq(x, ones_ref):
    sq = x * x
    hi = sq.astype(jnp.bfloat16)
    lo = (sq - hi.astype(jnp.float32)).astype(jnp.bfloat16)
    outs = []
    for c in range(x.shape[1] // MXU_DIM):
        sl = slice(c * MXU_DIM, (c + 1) * MXU_DIM)
        outs.append(_dot(hi[:, sl], ones_ref[...]) + _dot(lo[:, sl], ones_ref[...]))
    return jnp.concatenate(outs, axis=1)


def _mixer_in_kernel(x_ref, g_ref, wmain_ref, wf_ref, wgate_ref, bgate_ref,
                     convw_ref, convb_ref, wax_ref, ba_ref, bx_ref, lam_ref,
                     gq_ref, gk_ref, bf_ref, wlru_ref, ones_ref,
                     q_ref, k_ref, v_ref, cum_ref, p_ref, ga_ref,
                     conv_tail, h_carry, cum_carry):
    tm = x_ref.shape[1]

    @pl.when(pl.program_id(1) == 0)
    def _():
        conv_tail[...] = jnp.zeros_like(conv_tail)
        h_carry[...] = jnp.zeros_like(h_carry)
        cum_carry[...] = jnp.zeros_like(cum_carry)

    x = x_ref[0]
    h = _rms_norm(x, g_ref[...]).astype(jnp.bfloat16)

    z_lx = _dot(h, wmain_ref[:, 0:D_RNN])
    tail = conv_tail[...]
    rows8 = lax.broadcasted_iota(jnp.int32, tail.shape, 0)
    xr = z_lx * convw_ref[CONV_WIDTH - 1:CONV_WIDTH, :] + convb_ref[...]
    for j in range(1, CONV_WIDTH):
        rolled = pltpu.roll(z_lx, j, axis=0)
        head = jnp.where(rows8 < j, pltpu.roll(tail, j, axis=0), rolled[:SUBLANES])
        shifted = jnp.concatenate([head, rolled[SUBLANES:]], axis=0)
        xr = xr + shifted * convw_ref[CONV_WIDTH - 1 - j:CONV_WIDTH - j, :]
    conv_tail[...] = z_lx[tm - SUBLANES:, :]

    xr_b = xr.astype(jnp.bfloat16)
    r_parts, i_parts = [], []
    for n in range(N_RNN_BLOCKS):
        ri = _dot(xr_b[:, n * RNN_BLOCK:(n + 1) * RNN_BLOCK], wax_ref[n])
        r_parts.append(ri[:, :RNN_BLOCK])
        i_parts.append(ri[:, RNN_BLOCK:])
    r = jax.nn.sigmoid(jnp.concatenate(r_parts, axis=1) + ba_ref[...])
    i_gate = jax.nn.sigmoid(jnp.concatenate(i_parts, axis=1) + bx_ref[...])
    neg_lam = -lam_ref[...]
    softplus = jnp.maximum(neg_lam, 0.0) + jnp.log1p(jnp.exp(-jnp.abs(neg_lam)))
    log_a = -LRU_C * r * softplus
    a = jnp.exp(log_a)
    th = jnp.tanh(log_a)
    b = jnp.sqrt(-2.0 * th / (1.0 - th)) * (i_gate * xr)
    d = 1
    while d < tm:
        a_s = _shift_rows(a, d, 1.0)
        b_s = _shift_rows(b, d, 0.0)
        b = a * b_s + b
        a = a * a_s
        d *= 2
    hr = a * h_carry[...] + b
    h_carry[...] = hr[tm - 1:tm, :]

    z_lg = _dot(h, wmain_ref[:, D_RNN:2 * D_RNN])
    y = (jax.nn.gelu(z_lg) * hr).astype(jnp.bfloat16)
    y_lru = _dot(y, wlru_ref[...])

    gates = jax.nn.sigmoid(_dot(h, wgate_ref[...]) + bgate_ref[...])
    p_ref[0] = gates[:, :D_MODEL] * y_lru
    ga_ref[0] = gates[:, D_MODEL:]

    off = 2 * D_RNN
    q = _dot(h, wmain_ref[:, off:off + D_ATTN])
    q_ref[0] = (q * lax.rsqrt(_group_mean_sq(q, ones_ref) + EPS) * gq_ref[...]).astype(jnp.bfloat16)
    k = _dot(h, wmain_ref[:, off + D_ATTN:off + 2 * D_ATTN])
    k_ref[0] = (k * lax.rsqrt(_group_mean_sq(k, ones_ref) + EPS) * gk_ref[...]).astype(jnp.bfloat16)
    v_ref[0] = _dot(h, wmain_ref[:, off + 2 * D_ATTN:off + 3 * D_ATTN]).astype(jnp.bfloat16)

    f = _dot(h, wf_ref[...]) + bf_ref[...]
    c = -(jnp.maximum(-f, 0.0) + jnp.log1p(jnp.exp(-jnp.abs(f))))
    d = 1
    while d < tm:
        c = c + _shift_rows(c, d, 0.0)
        d *= 2
    c = c + cum_carry[...]
    cum_carry[...] = c[tm - 1:tm, :]
    cum_ref[0] = c


def _mixer_in(x, g, wmain, wf, wgate, bgate, convw, convb, wax, ba, bx, lam,
              gq, gk, bf, wlru, ones):
    bsz, seq, _ = x.shape
    tile = lambda w: pl.BlockSpec((1, MIX_TM, w), lambda b, t: (b, t, 0))
    consts = (g, wmain, wf, wgate, bgate, convw, convb, wax, ba, bx, lam, gq, gk, bf, wlru, ones)
    act = lambda w, dt: jax.ShapeDtypeStruct((bsz, seq, w), dt)
    return pl.pallas_call(
        _mixer_in_kernel,
        grid=(bsz, seq // MIX_TM),
        in_specs=[tile(D_MODEL)] + [_const_spec(c.shape) for c in consts],
        out_specs=[tile(D_ATTN), tile(D_ATTN), tile(D_ATTN), tile(LANES),
                   tile(D_MODEL), tile(D_MODEL)],
        out_shape=[act(D_ATTN, jnp.bfloat16), act(D_ATTN, jnp.bfloat16),
                   act(D_ATTN, jnp.bfloat16), act(LANES, jnp.float32),
                   act(D_MODEL, jnp.float32), act(D_MODEL, jnp.float32)],
        scratch_shapes=[pltpu.VMEM((SUBLANES, D_RNN), jnp.float32),
                        pltpu.VMEM((1, D_RNN), jnp.float32),
                        pltpu.VMEM((1, LANES), jnp.float32)],
        compiler_params=pltpu.CompilerParams(
            dimension_semantics=("parallel", "arbitrary"), vmem_limit_bytes=VMEM_LIMIT),
        name="mixer_in",
    )(x, *consts)


def _attn_kernel(q_ref, k_ref, v_ref, cq_ref, ck_ref, o_ref):
    qi = pl.program_id(2)
    t = ATT_T
    scale = HEAD_DIM ** -0.5
    q = q_ref[0]
    lane = lax.broadcasted_iota(jnp.int32, q.shape, 1)
    zero = jnp.zeros_like(q)
    q_heads = [jnp.where(lane < HEAD_DIM, q, zero), jnp.where(lane >= HEAD_DIM, q, zero)]
    cq = cq_ref[0, 0]
    cq_heads = [cq[:, i:i + 1] for i in range(HEADS_PER_STEP)]

    def step(kj, carry, masked):
        start = pl.multiple_of(kj * t, t)
        k = k_ref[0, pl.ds(start, t), :]
        v = v_ref[0, pl.ds(start, t), :]
        ck = ck_ref[0, 0, :, pl.ds(start, t)]
        out = []
        for i in range(HEADS_PER_STEP):
            m, l, acc = carry[i]
            s = lax.dot_general(q_heads[i], k, (((1,), (1,)), ((), ())),
                                preferred_element_type=jnp.float32) * scale
            s = s + cq_heads[i] - ck[i:i + 1, :]
            if masked:
                row = lax.broadcasted_iota(jnp.int32, s.shape, 0)
                col = lax.broadcasted_iota(jnp.int32, s.shape, 1)
                s = jnp.where(row >= col, s, -jnp.inf)
            m_new = jnp.maximum(m, jnp.max(s, axis=-1, keepdims=True))
            alpha = jnp.exp(m - m_new)
            p = jnp.exp(s - m_new)
            l = alpha * l + jnp.sum(p, axis=-1, keepdims=True)
            acc = alpha * acc + _dot(p.astype(jnp.bfloat16), v)
            out.append((m_new, l, acc))
        return tuple(out)

    init = tuple((jnp.full((t, 1), -jnp.inf, jnp.float32),
                  jnp.zeros((t, 1), jnp.float32),
                  jnp.zeros((t, LANES), jnp.float32)) for _ in range(HEADS_PER_STEP))
    carry = lax.fori_loop(0, qi, functools.partial(step, masked=False), init)
    carry = step(qi, carry, masked=True)
    o0 = carry[0][2] / carry[0][1]
    o1 = carry[1][2] / carry[1][1]
    o_ref[0] = jnp.where(lane < HEAD_DIM, o0, o1).astype(jnp.bfloat16)


def _attention(q, k, v, cq, ck):
    bsz, seq, _ = q.shape
    n_pairs = N_HEADS // HEADS_PER_STEP
    qspec = pl.BlockSpec((1, ATT_T, LANES), lambda b, p, i: (b, i, p))
    kvspec = pl.BlockSpec((1, seq, LANES), lambda b, p, i: (b, 0, p))
    return pl.pallas_call(
        _attn_kernel,
        grid=(bsz, n_pairs, seq // ATT_T),
        in_specs=[qspec, kvspec, kvspec,
                  pl.BlockSpec((1, 1, ATT_T, HEADS_PER_STEP), lambda b, p, i: (b, p, i, 0)),
                  pl.BlockSpec((1, 1, HEADS_PER_STEP, seq), lambda b, p, i: (b, p, 0, 0))],
        out_specs=qspec,
        out_shape=jax.ShapeDtypeStruct(q.shape, jnp.bfloat16),
        compiler_params=pltpu.CompilerParams(
            dimension_semantics=("parallel", "parallel", "arbitrary"),
            vmem_limit_bytes=VMEM_LIMIT),
        name="attention",
    )(q, k, v, cq, ck)


def _mixer_out_kernel(x_ref, o_ref, p_ref, ga_ref, wattn_ref, wo_ref,
                      g_ref, wg_ref, wu_ref, wd_ref, out_ref):
    y_attn = _dot(o_ref[...], wattn_ref[...])
    m = (p_ref[...] + ga_ref[...] * y_attn).astype(jnp.bfloat16)
    x = x_ref[...] + _dot(m, wo_ref[...])
    h = _rms_norm(x, g_ref[...]).astype(jnp.bfloat16)
    out_ref[...] = x + 0.5 * _swiglu(h, wg_ref, wu_ref, wd_ref)


def _mixer_out(x2d, o2d, p2d, ga2d, wattn, wo, g, wg, wu, wd):
    n = x2d.shape[0]
    row = pl.BlockSpec((FFN_TM, D_MODEL), lambda i: (i, 0))
    consts = (wattn, wo, g, wg, wu, wd)
    return pl.pallas_call(
        _mixer_out_kernel,
        grid=(n // FFN_TM,),
        in_specs=[row, row, row, row] + [_const_spec(c.shape) for c in consts],
        out_specs=row,
        out_shape=jax.ShapeDtypeStruct(x2d.shape, jnp.float32),
        compiler_params=pltpu.CompilerParams(
            dimension_semantics=("parallel",), vmem_limit_bytes=VMEM_LIMIT),
        name="mixer_out",
    )(x2d, o2d, p2d, ga2d, *consts)


def kernel(x, g_ffn1, w_up1, w_down1, g_mix, w_in, b_gate, conv_w, conv_b, w_a, b_a, w_x, b_x, lam, g_q, g_k, b_forget, w_lru_out, w_attn_out, w_o, g_ffn2, w_up2, w_down2):
    bsz, seq, _ = x.shape
    depth = g_ffn1.shape[0]
    n_pairs = N_HEADS // HEADS_PER_STEP
    bf16 = lambda w: w.astype(jnp.bfloat16)
    row = lambda p: p.reshape(1, -1)
    gid = jnp.arange(MXU_DIM) // HEAD_DIM
    ones = bf16(jnp.where(gid[:, None] == gid[None, :], 1.0 / HEAD_DIM, 0.0))

    x2d = x.reshape(bsz * seq, D_MODEL)
    for l in range(depth):
        x2d = _ffn(x2d, row(g_ffn1[l]), bf16(w_up1[l][:, :D_FF]), bf16(w_up1[l][:, D_FF:]),
                   bf16(w_down1[l]))

        wf = jnp.pad(w_in[l][:, OFF_F:OFF_GATE], ((0, 0), (0, LANES - N_HEADS)))
        bf = jnp.pad(b_forget[l], (0, LANES - N_HEADS))
        wax = jnp.concatenate([w_a[l], w_x[l]], axis=-1)
        q, k, v, cum, p, ga = _mixer_in(
            x2d.reshape(bsz, seq, D_MODEL), row(g_mix[l]), bf16(w_in[l][:, :OFF_F]), bf16(wf),
            bf16(w_in[l][:, OFF_GATE:]), row(b_gate[l]), conv_w[l], row(conv_b[l]), bf16(wax),
            row(b_a[l]), row(b_x[l]), row(lam[l]), row(jnp.tile(g_q[l], N_HEADS)),
            row(jnp.tile(g_k[l], N_HEADS)), row(bf), bf16(w_lru_out[l]), ones)

        cum_h = cum[:, :, :N_HEADS].reshape(bsz, seq, n_pairs, HEADS_PER_STEP)
        cq = cum_h.transpose(0, 2, 1, 3)
        ck = cum_h.transpose(0, 2, 3, 1)
        o = _attention(q, k, v, cq, ck)

        flat = lambda a: a.reshape(bsz * seq, -1)
        x2d = _mixer_out(x2d, flat(o), flat(p), flat(ga), bf16(w_attn_out[l]), bf16(w_o[l]),
                         row(g_ffn2[l]), bf16(w_up2[l][:, :D_FF]), bf16(w_up2[l][:, D_FF:]),
                         bf16(w_down2[l]))
    return x2d.reshape(bsz, seq, D_MODEL)
```

```python
import functools
import math

import jax
import jax.numpy as jnp
from jax import lax
from jax.experimental import pallas as pl
from jax.experimental.pallas import tpu as pltpu

D_MODEL = 1024
D_RNN = 1024
N_RNN_BLOCKS = 8
RNN_BLOCK = D_RNN // N_RNN_BLOCKS
CONV_WIDTH = 4
LRU_C = 8.0
N_HEADS = 16
HEAD_DIM = 64
D_ATTN = N_HEADS * HEAD_DIM
D_FF = 2816
EPS = 1e-6
OFF_F = 2 * D_RNN + 3 * D_ATTN
OFF_GATE = OFF_F + N_HEADS

LANES = 128
SUBLANES = 8
MXU_DIM = 256
VMEM_LIMIT = 56 * 1024 * 1024

FFN_TM = 256
FFN_FC = 256
MIX_TM = 256
ATT_T = 512
HEADS_PER_STEP = LANES // HEAD_DIM


def _dot(a, b):
    return jnp.dot(a, b, preferred_element_type=jnp.float32)


def _rms_norm(x, g):
    ms = jnp.mean(x * x, axis=-1, keepdims=True)
    return x * lax.rsqrt(ms + EPS) * g


def _const_spec(shape):
    nd = len(shape)
    return pl.BlockSpec(shape, lambda *_: (0,) * nd, pipeline_mode=pl.Buffered(1))


def _swiglu(h, wg_ref, wu_ref, wd_ref):
    acc = jnp.zeros((h.shape[0], D_MODEL), jnp.float32)
    for c in range(D_FF // FFN_FC):
        sl = slice(c * FFN_FC, (c + 1) * FFN_FC)
        gate = _dot(h, wg_ref[:, sl])
        up = _dot(h, wu_ref[:, sl])
        act = (gate * jax.nn.sigmoid(gate) * up).astype(jnp.bfloat16)
        acc = acc + _dot(act, wd_ref[sl, :])
    return acc


def _ffn_kernel(x_ref, g_ref, wg_ref, wu_ref, wd_ref, o_ref):
    x = x_ref[...]
    h = _rms_norm(x, g_ref[...]).astype(jnp.bfloat16)
    o_ref[...] = x + 0.5 * _swiglu(h, wg_ref, wu_ref, wd_ref)


def _ffn(x2d, g, wg, wu, wd):
    n = x2d.shape[0]
    row = pl.BlockSpec((FFN_TM, D_MODEL), lambda i: (i, 0))
    return pl.pallas_call(
        _ffn_kernel,
        grid=(n // FFN_TM,),
        in_specs=[row, _const_spec(g.shape), _const_spec(wg.shape),
                  _const_spec(wu.shape), _const_spec(wd.shape)],
        out_specs=row,
        out_shape=jax.ShapeDtypeStruct(x2d.shape, jnp.float32),
        compiler_params=pltpu.CompilerParams(
            dimension_semantics=("parallel",), vmem_limit_bytes=VMEM_LIMIT),
        name="ffn",
    )(x2d, g, wg, wu, wd)


def _shift_rows(x, d, fill):
    rows = lax.broadcasted_iota(jnp.int32, x.shape, 0)
    return jnp.where(rows >= d, pltpu.roll(x, d, axis=0), fill)


def _group_mean_sq(x, ones_ref):
    sq = x * x
    hi = sq.astype(jnp.bfloat16)
    lo = (sq - hi.astype(jnp.float32)).astype(jnp.bfloat16)
    outs = []
    for c in range(x.shape[1] // MXU_DIM):
        sl = slice(c * MXU_DIM, (c + 1) * MXU_DIM)
        outs.append(_dot(hi[:, sl], ones_ref[...]) + _dot(lo[:, sl], ones_ref[...]))
    return jnp.concatenate(outs, axis=1)


def _mixer_in_kernel(x_ref, g_ref, wmain_ref, wf_ref, wgate_ref, bgate_ref,
                     convw_ref, convb_ref, wax_ref, ba_ref, bx_ref, lam_ref,
                     gq_ref, gk_ref, bf_ref, wlru_ref, ones_ref,
                     q_ref, k_ref, v_ref, cum_ref, p_ref, ga_ref,
                     conv_tail, h_carry, cum_carry):
    tm = x_ref.shape[1]

    @pl.when(pl.program_id(1) == 0)
    def _():
        conv_tail[...] = jnp.zeros_like(conv_tail)
        h_carry[...] = jnp.zeros_like(h_carry)
        cum_carry[...] = jnp.zeros_like(cum_carry)

    x = x_ref[0]
    h = _rms_norm(x, g_ref[...]).astype(jnp.bfloat16)

    z_lx = _dot(h, wmain_ref[:, 0:D_RNN])
    tail = conv_tail[...]
    rows8 = lax.broadcasted_iota(jnp.int32, tail.shape, 0)
    xr = z_lx * convw_ref[CONV_WIDTH - 1:CONV_WIDTH, :] + convb_ref[...]
    for j in range(1, CONV_WIDTH):
        rolled = pltpu.roll(z_lx, j, axis=0)
        head = jnp.where(rows8 < j, pltpu.roll(tail, j, axis=0), rolled[:SUBLANES])
        shifted = jnp.concatenate([head, rolled[SUBLANES:]], axis=0)
        xr = xr + shifted * convw_ref[CONV_WIDTH - 1 - j:CONV_WIDTH - j, :]
    conv_tail[...] = z_lx[tm - SUBLANES:, :]

    xr_b = xr.astype(jnp.bfloat16)
    r_parts, i_parts = [], []
    for n in range(N_RNN_BLOCKS):
        ri = _dot(xr_b[:, n * RNN_BLOCK:(n + 1) * RNN_BLOCK], wax_ref[n])
        r_parts.append(ri[:, :RNN_BLOCK])
        i_parts.append(ri[:, RNN_BLOCK:])
    r = jax.nn.sigmoid(jnp.concatenate(r_parts, axis=1) + ba_ref[...])
    i_gate = jax.nn.sigmoid(jnp.concatenate(i_parts, axis=1) + bx_ref[...])
    neg_lam = -lam_ref[...]
    softplus = jnp.maximum(neg_lam, 0.0) + jnp.log1p(jnp.exp(-jnp.abs(neg_lam)))
    log_a = -LRU_C * r * softplus
    a = jnp.exp(log_a)
    th = jnp.tanh(log_a)
    b = jnp.sqrt(-2.0 * th / (1.0 - th)) * (i_gate * xr)
    d = 1
    while d < tm:
        a_s = _shift_rows(a, d, 1.0)
        b_s = _shift_rows(b, d, 0.0)
        b = a * b_s + b
        a = a * a_s
        d *= 2
    hr = a * h_carry[...] + b
    h_carry[...] = hr[tm - 1:tm, :]

    z_lg = _dot(h, wmain_ref[:, D_RNN:2 * D_RNN])
    y = (jax.nn.gelu(z_lg) * hr).astype(jnp.bfloat16)
    y_lru = _dot(y, wlru_ref[...])

    gates = jax.nn.sigmoid(_dot(h, wgate_ref[...]) + bgate_ref[...])
    p_ref[0] = gates[:, :D_MODEL] * y_lru
    ga_ref[0] = gates[:, D_MODEL:]

    off = 2 * D_RNN
    q = _dot(h, wmain_ref[:, off:off + D_ATTN])
    q_ref[0] = (q * lax.rsqrt(_group_mean_sq(q, ones_ref) + EPS) * gq_ref[...]).astype(jnp.bfloat16)
    k = _dot(h, wmain_ref[:, off + D_ATTN:off + 2 * D_ATTN])
    k_ref[0] = (k * lax.rsqrt(_group_mean_sq(k, ones_ref) + EPS) * gk_ref[...]).astype(jnp.bfloat16)
    v_ref[0] = _dot(h, wmain_ref[:, off + 2 * D_ATTN:off + 3 * D_ATTN]).astype(jnp.bfloat16)

    f = _dot(h, wf_ref[...]) + bf_ref[...]
    c = -(jnp.maximum(-f, 0.0) + jnp.log1p(jnp.exp(-jnp.abs(f))))
    d = 1
    while d < tm:
        c = c + _shift_rows(c, d, 0.0)
        d *= 2
    c = c + cum_carry[...]
    cum_carry[...] = c[tm - 1:tm, :]
    cum_ref[0] = c


def _mixer_in(x, g, wmain, wf, wgate, bgate, convw, convb, wax, ba, bx, lam,
              gq, gk, bf, wlru, ones):
    bsz, seq, _ = x.shape
    tile = lambda w: pl.BlockSpec((1, MIX_TM, w), lambda b, t: (b, t, 0))
    consts = (g, wmain, wf, wgate, bgate, convw, convb, wax, ba, bx, lam, gq, gk, bf, wlru, ones)
    act = lambda w, dt: jax.ShapeDtypeStruct((bsz, seq, w), dt)
    return pl.pallas_call(
        _mixer_in_kernel,
        grid=(bsz, seq // MIX_TM),
        in_specs=[tile(D_MODEL)] + [_const_spec(c.shape) for c in consts],
        out_specs=[tile(D_ATTN), tile(D_ATTN), tile(D_ATTN), tile(LANES),
                   tile(D_MODEL), tile(D_MODEL)],
        out_shape=[act(D_ATTN, jnp.bfloat16), act(D_ATTN, jnp.bfloat16),
                   act(D_ATTN, jnp.bfloat16), act(LANES, jnp.float32),
                   act(D_MODEL, jnp.float32), act(D_MODEL, jnp.float32)],
        scratch_shapes=[pltpu.VMEM((SUBLANES, D_RNN), jnp.float32),
                        pltpu.VMEM((1, D_RNN), jnp.float32),
                        pltpu.VMEM((1, LANES), jnp.float32)],
        compiler_params=pltpu.CompilerParams(
            dimension_semantics=("parallel", "arbitrary"), vmem_limit_bytes=VMEM_LIMIT),
        name="mixer_in",
    )(x, *consts)


N_SPLIT = 3


def _split_bf16(x):
    parts = []
    for _ in range(N_SPLIT):
        t = x.astype(jnp.bfloat16).astype(jnp.float32)
        parts.append(t)
        x = x - t
    return parts


def _augment(x, head, cols):
    lane = lax.broadcasted_iota(jnp.int32, x.shape, 1)
    base = HEAD_DIM * (1 - head)
    aug = jnp.zeros(x.shape, jnp.float32)
    for i, c in enumerate(cols):
        aug = jnp.where(lane == base + i, c, aug)
    own = (lane < HEAD_DIM) if head == 0 else (lane >= HEAD_DIM)
    return jnp.where(own, x, aug.astype(x.dtype))


def _attn_kernel(q_ref, k_ref, v_ref, cq_ref, ck_ref, o_ref, kaug, vaug):
    qi = pl.program_id(2)
    t = ATT_T
    seq = k_ref.shape[1]
    ones = [1.0] * N_SPLIT

    @pl.when(qi == 0)
    def _():
        def build(c, _):
            rows = pl.ds(pl.multiple_of(c * t, t), t)
            k = k_ref[0, rows, :]
            v = v_ref[0, rows, :]
            ck = ck_ref[0, 0, rows, :]
            for h in range(HEADS_PER_STEP):
                neg = [-p for p in _split_bf16(ck[:, h:h + 1])]
                kaug[h, rows, :] = _augment(k, h, ones + neg)
                vaug[h, rows, :] = _augment(v, h, [1.0])
            return 0
        lax.fori_loop(0, seq // t, build, 0)

    q = q_ref[0] * jnp.bfloat16(HEAD_DIM ** -0.5)
    cq = cq_ref[0, 0]
    qa = [_augment(q, h, _split_bf16(cq[:, h:h + 1]) + ones) for h in range(HEADS_PER_STEP)]
    row = lax.broadcasted_iota(jnp.int32, (t, t), 0)
    col = lax.broadcasted_iota(jnp.int32, (t, t), 1)
    causal = row >= col

    def step(kj, carry, masked):
        rows = pl.ds(pl.multiple_of(kj * t, t), t)
        out = []
        for h in range(HEADS_PER_STEP):
            m, acc = carry[h]
            s = lax.dot_general(qa[h], kaug[h, rows, :], (((1,), (1,)), ((), ())),
                                preferred_element_type=jnp.float32)
            if masked:
                s = jnp.where(causal, s, -jnp.inf)
            m_new = jnp.maximum(m, jnp.max(s, axis=-1, keepdims=True))
            p = jnp.exp(s - m_new).astype(jnp.bfloat16)
            acc = jnp.exp(m - m_new) * acc + _dot(p, vaug[h, rows, :])
            out.append((m_new, acc))
        return tuple(out)

    init = tuple((jnp.full((t, 1), -jnp.inf, jnp.float32), jnp.zeros((t, LANES), jnp.float32))
                 for _ in range(HEADS_PER_STEP))
    carry = lax.fori_loop(0, qi, functools.partial(step, masked=False), init)
    carry = step(qi, carry, masked=True)
    lane = lax.broadcasted_iota(jnp.int32, (t, LANES), 1)
    o0 = carry[0][1] / carry[0][1][:, HEAD_DIM:HEAD_DIM + 1]
    o1 = carry[1][1] / carry[1][1][:, 0:1]
    o_ref[0] = jnp.where(lane < HEAD_DIM, o0, o1).astype(jnp.bfloat16)


def _attention(q, k, v, cum_cols):
    bsz, seq, _ = q.shape
    n_pairs = N_HEADS // HEADS_PER_STEP
    qspec = pl.BlockSpec((1, ATT_T, LANES), lambda b, p, i: (b, i, p))
    kvspec = pl.BlockSpec((1, seq, LANES), lambda b, p, i: (b, 0, p))
    return pl.pallas_call(
        _attn_kernel,
        grid=(bsz, n_pairs, seq // ATT_T),
        in_specs=[qspec, kvspec, kvspec,
                  pl.BlockSpec((1, 1, ATT_T, HEADS_PER_STEP), lambda b, p, i: (b, p, i, 0)),
                  pl.BlockSpec((1, 1, seq, HEADS_PER_STEP), lambda b, p, i: (b, p, 0, 0))],
        out_specs=qspec,
        out_shape=jax.ShapeDtypeStruct(q.shape, jnp.bfloat16),
        scratch_shapes=[pltpu.VMEM((HEADS_PER_STEP, seq, LANES), jnp.bfloat16),
                        pltpu.VMEM((HEADS_PER_STEP, seq, LANES), jnp.bfloat16)],
        compiler_params=pltpu.CompilerParams(
            dimension_semantics=("parallel", "parallel", "arbitrary"),
            vmem_limit_bytes=VMEM_LIMIT),
        name="attention",
    )(q, k, v, cum_cols, cum_cols)


def _mixer_out_kernel(x_ref, o_ref, p_ref, ga_ref, wattn_ref, wo_ref,
                      g_ref, wg_ref, wu_ref, wd_ref, out_ref):
    y_attn = _dot(o_ref[...], wattn_ref[...])
    m = (p_ref[...] + ga_ref[...] * y_attn).astype(jnp.bfloat16)
    x = x_ref[...] + _dot(m, wo_ref[...])
    h = _rms_norm(x, g_ref[...]).astype(jnp.bfloat16)
    out_ref[...] = x + 0.5 * _swiglu(h, wg_ref, wu_ref, wd_ref)


def _mixer_out(x2d, o2d, p2d, ga2d, wattn, wo, g, wg, wu, wd):
    n = x2d.shape[0]
    row = pl.BlockSpec((FFN_TM, D_MODEL), lambda i: (i, 0))
    consts = (wattn, wo, g, wg, wu, wd)
    return pl.pallas_call(
        _mixer_out_kernel,
        grid=(n // FFN_TM,),
        in_specs=[row, row, row, row] + [_const_spec(c.shape) for c in consts],
        out_specs=row,
        out_shape=jax.ShapeDtypeStruct(x2d.shape, jnp.float32),
        compiler_params=pltpu.CompilerParams(
            dimension_semantics=("parallel",), vmem_limit_bytes=VMEM_LIMIT),
        name="mixer_out",
    )(x2d, o2d, p2d, ga2d, *consts)


def kernel(x, g_ffn1, w_up1, w_down1, g_mix, w_in, b_gate, conv_w, conv_b, w_a, b_a, w_x, b_x, lam, g_q, g_k, b_forget, w_lru_out, w_attn_out, w_o, g_ffn2, w_up2, w_down2):
    bsz, seq, _ = x.shape
    depth = g_ffn1.shape[0]
    n_pairs = N_HEADS // HEADS_PER_STEP
    bf16 = lambda w: w.astype(jnp.bfloat16)
    row = lambda p: p.reshape(1, -1)
    gid = jnp.arange(MXU_DIM) // HEAD_DIM
    ones = bf16(jnp.where(gid[:, None] == gid[None, :], 1.0 / HEAD_DIM, 0.0))

    x2d = x.reshape(bsz * seq, D_MODEL)
    for l in range(depth):
        x2d = _ffn(x2d, row(g_ffn1[l]), bf16(w_up1[l][:, :D_FF]), bf16(w_up1[l][:, D_FF:]),
                   bf16(w_down1[l]))

        wf = jnp.pad(w_in[l][:, OFF_F:OFF_GATE], ((0, 0), (0, LANES - N_HEADS)))
        bf = jnp.pad(b_forget[l], (0, LANES - N_HEADS))
        wax = jnp.concatenate([w_a[l], w_x[l]], axis=-1)
        q, k, v, cum, p, ga = _mixer_in(
            x2d.reshape(bsz, seq, D_MODEL), row(g_mix[l]), bf16(w_in[l][:, :OFF_F]), bf16(wf),
            bf16(w_in[l][:, OFF_GATE:]), row(b_gate[l]), conv_w[l], row(conv_b[l]), bf16(wax),
            row(b_a[l]), row(b_x[l]), row(lam[l]), row(jnp.tile(g_q[l], N_HEADS)),
            row(jnp.tile(g_k[l], N_HEADS)), row(bf), bf16(w_lru_out[l]), ones)

        cum_h = cum[:, :, :N_HEADS].reshape(bsz, seq, n_pairs, HEADS_PER_STEP)
        o = _attention(q, k, v, cum_h.transpose(0, 2, 1, 3))

        flat = lambda a: a.reshape(bsz * seq, -1)
        x2d = _mixer_out(x2d, flat(o), flat(p), flat(ga), bf16(w_attn_out[l]), bf16(w_o[l]),
                         row(g_ffn2[l]), bf16(w_up2[l][:, :D_FF]), bf16(w_up2[l][:, D_FF:]),
                         bf16(w_down2[l]))
    return x2d.reshape(bsz, seq, D_MODEL)
```

```python
import jax
import jax.numpy as jnp
from jax import lax
from jax.experimental import pallas as pl
from jax.experimental.pallas import tpu as pltpu

D_MODEL = 1024
D_RNN = 1024
N_RNN_BLOCKS = 8
RNN_BLOCK = D_RNN // N_RNN_BLOCKS
CONV_WIDTH = 4
LRU_C = 8.0
N_HEADS = 16
HEAD_DIM = 64
D_ATTN = N_HEADS * HEAD_DIM
D_FF = 2816
EPS = 1e-6
OFF_F = 2 * D_RNN + 3 * D_ATTN
OFF_GATE = OFF_F + N_HEADS

LANES = 128
SUBLANES = 8
MXU_DIM = 256
VMEM_LIMIT = 56 * 1024 * 1024

FFN_TM = 512
FFN_FC = 256
MIX_TM = 512
ATT_T = 512
HEADS_PER_STEP = LANES // HEAD_DIM


def _dot(a, b):
    return jnp.dot(a, b, preferred_element_type=jnp.float32)


def _rms_norm(x, g):
    ms = jnp.mean(x * x, axis=-1, keepdims=True)
    return x * lax.rsqrt(ms + EPS) * g


def _const_spec(shape):
    nd = len(shape)
    return pl.BlockSpec(shape, lambda *_: (0,) * nd, pipeline_mode=pl.Buffered(1))


def _swiglu(h, wg_ref, wu_ref, wd_ref):
    acc = jnp.zeros((h.shape[0], D_MODEL), jnp.float32)
    for c in range(D_FF // FFN_FC):
        sl = slice(c * FFN_FC, (c + 1) * FFN_FC)
        gate = _dot(h, wg_ref[:, sl])
        up = _dot(h, wu_ref[:, sl])
        act = (gate * jax.nn.sigmoid(gate) * up).astype(jnp.bfloat16)
        acc = acc + _dot(act, wd_ref[sl, :])
    return acc


def _ffn_kernel(x_ref, g_ref, wg_ref, wu_ref, wd_ref, o_ref):
    x = x_ref[...]
    h = _rms_norm(x, g_ref[...]).astype(jnp.bfloat16)
    o_ref[...] = x + 0.5 * _swiglu(h, wg_ref, wu_ref, wd_ref)


def _ffn(x2d, g, wg, wu, wd):
    n = x2d.shape[0]
    row = pl.BlockSpec((FFN_TM, D_MODEL), lambda i: (i, 0))
    return pl.pallas_call(
        _ffn_kernel,
        grid=(n // FFN_TM,),
        in_specs=[row, _const_spec(g.shape), _const_spec(wg.shape),
                  _const_spec(wu.shape), _const_spec(wd.shape)],
        out_specs=row,
        out_shape=jax.ShapeDtypeStruct(x2d.shape, jnp.float32),
        compiler_params=pltpu.CompilerParams(
            dimension_semantics=("parallel",), vmem_limit_bytes=VMEM_LIMIT),
        name="ffn",
    )(x2d, g, wg, wu, wd)


def _shift_rows(x, d, fill):
    rows = lax.broadcasted_iota(jnp.int32, x.shape, 0)
    return jnp.where(rows >= d, pltpu.roll(x, d, axis=0), fill)


def _linear_scan(a, b, h0):
    tm, c = a.shape
    groups = tm // SUBLANES
    a3 = a.reshape(groups, SUBLANES, c)
    b3 = b.reshape(groups, SUBLANES, c)
    r = lax.broadcasted_iota(jnp.int32, a3.shape, 1)
    d = 1
    while d < SUBLANES:
        keep = r >= d
        a_s = jnp.where(keep, pltpu.roll(a3, d, axis=1), 1.0)
        b_s = jnp.where(keep, pltpu.roll(b3, d, axis=1), 0.0)
        b3 = a3 * b_s + b3
        a3 = a3 * a_s
        d *= 2
    carry = h0
    out = []
    for i in range(groups):
        hg = a3[i] * carry + b3[i]
        out.append(hg)
        carry = hg[SUBLANES - 1:SUBLANES, :]
    return jnp.concatenate(out, axis=0), carry


def _group_mean_sq(x, ones_ref):
    sq = x * x
    hi = sq.astype(jnp.bfloat16)
    lo = (sq - hi.astype(jnp.float32)).astype(jnp.bfloat16)
    outs = []
    for c in range(x.shape[1] // MXU_DIM):
        sl = slice(c * MXU_DIM, (c + 1) * MXU_DIM)
        outs.append(_dot(hi[:, sl], ones_ref[...]) + _dot(lo[:, sl], ones_ref[...]))
    return jnp.concatenate(outs, axis=1)


def _mixer_in_kernel(x_ref, g_ref, wmain_ref, wf_ref, wgate_ref, bgate_ref,
                     convw_ref, convb_ref, wax_ref, ba_ref, bx_ref, lam_ref,
                     gq_ref, gk_ref, bf_ref, wlru_ref, ones_ref,
                     q_ref, k_ref, v_ref, cum_ref, p_ref, ga_ref,
                     conv_tail, h_carry, cum_carry):
    tm = x_ref.shape[1]

    @pl.when(pl.program_id(1) == 0)
    def _():
        conv_tail[...] = jnp.zeros_like(conv_tail)
        h_carry[...] = jnp.zeros_like(h_carry)
        cum_carry[...] = jnp.zeros_like(cum_carry)

    x = x_ref[0]
    h = _rms_norm(x, g_ref[...]).astype(jnp.bfloat16)

    z_lx = _dot(h, wmain_ref[:, 0:D_RNN])
    tail = conv_tail[...]
    rows8 = lax.broadcasted_iota(jnp.int32, tail.shape, 0)
    xr = z_lx * convw_ref[CONV_WIDTH - 1:CONV_WIDTH, :] + convb_ref[...]
    for j in range(1, CONV_WIDTH):
        rolled = pltpu.roll(z_lx, j, axis=0)
        head = jnp.where(rows8 < j, pltpu.roll(tail, j, axis=0), rolled[:SUBLANES])
        shifted = jnp.concatenate([head, rolled[SUBLANES:]], axis=0)
        xr = xr + shifted * convw_ref[CONV_WIDTH - 1 - j:CONV_WIDTH - j, :]
    conv_tail[...] = z_lx[tm - SUBLANES:, :]

    xr_b = xr.astype(jnp.bfloat16)
    r_parts, i_parts = [], []
    for n in range(N_RNN_BLOCKS):
        ri = _dot(xr_b[:, n * RNN_BLOCK:(n + 1) * RNN_BLOCK], wax_ref[n])
        r_parts.append(ri[:, :RNN_BLOCK])
        i_parts.append(ri[:, RNN_BLOCK:])
    r = jax.nn.sigmoid(jnp.concatenate(r_parts, axis=1) + ba_ref[...])
    i_gate = jax.nn.sigmoid(jnp.concatenate(i_parts, axis=1) + bx_ref[...])
    neg_lam = -lam_ref[...]
    softplus = jnp.maximum(neg_lam, 0.0) + jnp.log1p(jnp.exp(-jnp.abs(neg_lam)))
    log_a = -LRU_C * r * softplus
    a = jnp.exp(log_a)
    th = jnp.tanh(log_a)
    b = jnp.sqrt(-2.0 * th / (1.0 - th)) * (i_gate * xr)
    hr, h_last = _linear_scan(a, b, h_carry[...])
    h_carry[...] = h_last

    z_lg = _dot(h, wmain_ref[:, D_RNN:2 * D_RNN])
    y = (jax.nn.gelu(z_lg) * hr).astype(jnp.bfloat16)
    y_lru = _dot(y, wlru_ref[...])

    gates = jax.nn.sigmoid(_dot(h, wgate_ref[...]) + bgate_ref[...])
    p_ref[0] = gates[:, :D_MODEL] * y_lru
    ga_ref[0] = gates[:, D_MODEL:]

    off = 2 * D_RNN
    q = _dot(h, wmain_ref[:, off:off + D_ATTN])
    q_ref[0] = (q * lax.rsqrt(_group_mean_sq(q, ones_ref) + EPS) * gq_ref[...]).astype(jnp.bfloat16)
    k = _dot(h, wmain_ref[:, off + D_ATTN:off + 2 * D_ATTN])
    k_ref[0] = (k * lax.rsqrt(_group_mean_sq(k, ones_ref) + EPS) * gk_ref[...]).astype(jnp.bfloat16)
    v_ref[0] = _dot(h, wmain_ref[:, off + 2 * D_ATTN:off + 3 * D_ATTN]).astype(jnp.bfloat16)

    f = _dot(h, wf_ref[...]) + bf_ref[...]
    c = -(jnp.maximum(-f, 0.0) + jnp.log1p(jnp.exp(-jnp.abs(f))))
    d = 1
    while d < tm:
        c = c + _shift_rows(c, d, 0.0)
        d *= 2
    c = c + cum_carry[...]
    cum_carry[...] = c[tm - 1:tm, :]
    cum_ref[0] = c


def _mixer_in(x, g, wmain, wf, wgate, bgate, convw, convb, wax, ba, bx, lam,
              gq, gk, bf, wlru, ones):
    bsz, seq, _ = x.shape
    tile = lambda w: pl.BlockSpec((1, MIX_TM, w), lambda b, t: (b, t, 0))
    consts = (g, wmain, wf, wgate, bgate, convw, convb, wax, ba, bx, lam, gq, gk, bf, wlru, ones)
    act = lambda w, dt: jax.ShapeDtypeStruct((bsz, seq, w), dt)
    return pl.pallas_call(
        _mixer_in_kernel,
        grid=(bsz, seq // MIX_TM),
        in_specs=[tile(D_MODEL)] + [_const_spec(c.shape) for c in consts],
        out_specs=[tile(D_ATTN), tile(D_ATTN), tile(D_ATTN), tile(LANES),
                   tile(D_MODEL), tile(D_MODEL)],
        out_shape=[act(D_ATTN, jnp.bfloat16), act(D_ATTN, jnp.bfloat16),
                   act(D_ATTN, jnp.bfloat16), act(LANES, jnp.float32),
                   act(D_MODEL, jnp.float32), act(D_MODEL, jnp.float32)],
        scratch_shapes=[pltpu.VMEM((SUBLANES, D_RNN), jnp.float32),
                        pltpu.VMEM((1, D_RNN), jnp.float32),
                        pltpu.VMEM((1, LANES), jnp.float32)],
        compiler_params=pltpu.CompilerParams(
            dimension_semantics=("parallel", "arbitrary"), vmem_limit_bytes=VMEM_LIMIT),
        name="mixer_in",
    )(x, *consts)


N_SPLIT = 3


def _split_bf16(x):
    parts = []
    for _ in range(N_SPLIT):
        t = x.astype(jnp.bfloat16).astype(jnp.float32)
        parts.append(t)
        x = x - t
    return parts


def _augment(x, head, cols):
    lane = lax.broadcasted_iota(jnp.int32, x.shape, 1)
    base = HEAD_DIM * (1 - head)
    aug = jnp.zeros(x.shape, jnp.float32)
    for i, c in enumerate(cols):
        aug = jnp.where(lane == base + i, c, aug)
    own = (lane < HEAD_DIM) if head == 0 else (lane >= HEAD_DIM)
    return jnp.where(own, x, aug.astype(x.dtype))


def _attn_kernel(q_ref, k_ref, v_ref, cq_ref, ck_ref, o_ref, kaug, vaug):
    qi = pl.program_id(2)
    t = ATT_T
    seq = k_ref.shape[1]
    ones = [1.0] * N_SPLIT

    @pl.when(qi == 0)
    def _():
        def build(c, _):
            rows = pl.ds(pl.multiple_of(c * t, t), t)
            k = k_ref[0, rows, :]
            v = v_ref[0, rows, :]
            ck = ck_ref[0, 0, rows, :]
            for h in range(HEADS_PER_STEP):
                neg = [-p for p in _split_bf16(ck[:, h:h + 1])]
                kaug[h, rows, :] = _augment(k, h, ones + neg)
                vaug[h, rows, :] = _augment(v, h, [1.0])
            return 0
        lax.fori_loop(0, seq // t, build, 0)

    q = q_ref[0] * jnp.bfloat16(HEAD_DIM ** -0.5)
    cq = cq_ref[0, 0]
    qa = [_augment(q, h, _split_bf16(cq[:, h:h + 1]) + ones) for h in range(HEADS_PER_STEP)]
    row = lax.broadcasted_iota(jnp.int32, (t, t), 0)
    col = lax.broadcasted_iota(jnp.int32, (t, t), 1)
    causal = row >= col

    def step(kj, carry, masked):
        rows = pl.ds(kj * t, t)
        scores = []
        for h in range(HEADS_PER_STEP):
            s = lax.dot_general(qa[h], kaug[h, rows, :], (((1,), (1,)), ((), ())),
                                preferred_element_type=jnp.float32)
            scores.append(jnp.where(causal, s, -jnp.inf) if masked else s)
        probs, maxes, alphas = [], [], []
        for h in range(HEADS_PER_STEP):
            m = carry[h][0]
            m_new = jnp.maximum(m, jnp.max(scores[h], axis=-1, keepdims=True))
            probs.append(jnp.exp(scores[h] - m_new).astype(jnp.bfloat16))
            alphas.append(jnp.exp(m - m_new))
            maxes.append(m_new)
        return tuple((maxes[h], alphas[h] * carry[h][1] + _dot(probs[h], vaug[h, rows, :]))
                     for h in range(HEADS_PER_STEP))

    lane = lax.broadcasted_iota(jnp.int32, (t, LANES), 1)
    for n_full in range(seq // t):
        @pl.when(qi == n_full)
        def _(n_full=n_full):
            carry = tuple((jnp.full((t, 1), -jnp.inf, jnp.float32),
                           jnp.zeros((t, LANES), jnp.float32)) for _ in range(HEADS_PER_STEP))
            for kj in range(n_full):
                carry = step(kj, carry, masked=False)
            carry = step(n_full, carry, masked=True)
            o0 = carry[0][1] / carry[0][1][:, HEAD_DIM:HEAD_DIM + 1]
            o1 = carry[1][1] / carry[1][1][:, 0:1]
            o_ref[0] = jnp.where(lane < HEAD_DIM, o0, o1).astype(jnp.bfloat16)


def _attention(q, k, v, cum_cols):
    bsz, seq, _ = q.shape
    n_pairs = N_HEADS // HEADS_PER_STEP
    qspec = pl.BlockSpec((1, ATT_T, LANES), lambda b, p, i: (b, i, p))
    kvspec = pl.BlockSpec((1, seq, LANES), lambda b, p, i: (b, 0, p))
    return pl.pallas_call(
        _attn_kernel,
        grid=(bsz, n_pairs, seq // ATT_T),
        in_specs=[qspec, kvspec, kvspec,
                  pl.BlockSpec((1, 1, ATT_T, HEADS_PER_STEP), lambda b, p, i: (b, p, i, 0)),
                  pl.BlockSpec((1, 1, seq, HEADS_PER_STEP), lambda b, p, i: (b, p, 0, 0))],
        out_specs=qspec,
        out_shape=jax.ShapeDtypeStruct(q.shape, jnp.bfloat16),
        scratch_shapes=[pltpu.VMEM((HEADS_PER_STEP, seq, LANES), jnp.bfloat16),
                        pltpu.VMEM((HEADS_PER_STEP, seq, LANES), jnp.bfloat16)],
        compiler_params=pltpu.CompilerParams(
            dimension_semantics=("parallel", "parallel", "arbitrary"),
            vmem_limit_bytes=VMEM_LIMIT),
        name="attention",
    )(q, k, v, cum_cols, cum_cols)


def _mixer_out_kernel(x_ref, o_ref, p_ref, ga_ref, wattn_ref, wo_ref,
                      g_ref, wg_ref, wu_ref, wd_ref, out_ref):
    y_attn = _dot(o_ref[...], wattn_ref[...])
    m = (p_ref[...] + ga_ref[...] * y_attn).astype(jnp.bfloat16)
    x = x_ref[...] + _dot(m, wo_ref[...])
    h = _rms_norm(x, g_ref[...]).astype(jnp.bfloat16)
    out_ref[...] = x + 0.5 * _swiglu(h, wg_ref, wu_ref, wd_ref)


def _mixer_out(x2d, o2d, p2d, ga2d, wattn, wo, g, wg, wu, wd):
    n = x2d.shape[0]
    row = pl.BlockSpec((FFN_TM, D_MODEL), lambda i: (i, 0))
    consts = (wattn, wo, g, wg, wu, wd)
    return pl.pallas_call(
        _mixer_out_kernel,
        grid=(n // FFN_TM,),
        in_specs=[row, row, row, row] + [_const_spec(c.shape) for c in consts],
        out_specs=row,
        out_shape=jax.ShapeDtypeStruct(x2d.shape, jnp.float32),
        compiler_params=pltpu.CompilerParams(
            dimension_semantics=("parallel",), vmem_limit_bytes=VMEM_LIMIT),
        name="mixer_out",
    )(x2d, o2d, p2d, ga2d, *consts)


def kernel(x, g_ffn1, w_up1, w_down1, g_mix, w_in, b_gate, conv_w, conv_b, w_a, b_a, w_x, b_x, lam, g_q, g_k, b_forget, w_lru_out, w_attn_out, w_o, g_ffn2, w_up2, w_down2):
    bsz, seq, _ = x.shape
    depth = g_ffn1.shape[0]
    n_pairs = N_HEADS // HEADS_PER_STEP
    bf16 = lambda w: w.astype(jnp.bfloat16)
    row = lambda p: p.reshape(1, -1)
    gid = jnp.arange(MXU_DIM) // HEAD_DIM
    ones = bf16(jnp.where(gid[:, None] == gid[None, :], 1.0 / HEAD_DIM, 0.0))

    x2d = x.reshape(bsz * seq, D_MODEL)
    for l in range(depth):
        x2d = _ffn(x2d, row(g_ffn1[l]), bf16(w_up1[l][:, :D_FF]), bf16(w_up1[l][:, D_FF:]),
                   bf16(w_down1[l]))

        wf = jnp.pad(w_in[l][:, OFF_F:OFF_GATE], ((0, 0), (0, LANES - N_HEADS)))
        bf = jnp.pad(b_forget[l], (0, LANES - N_HEADS))
        wax = jnp.concatenate([w_a[l], w_x[l]], axis=-1)
        q, k, v, cum, p, ga = _mixer_in(
            x2d.reshape(bsz, seq, D_MODEL), row(g_mix[l]), bf16(w_in[l][:, :OFF_F]), bf16(wf),
            bf16(w_in[l][:, OFF_GATE:]), row(b_gate[l]), conv_w[l], row(conv_b[l]), bf16(wax),
            row(b_a[l]), row(b_x[l]), row(lam[l]), row(jnp.tile(g_q[l], N_HEADS)),
            row(jnp.tile(g_k[l], N_HEADS)), row(bf), bf16(w_lru_out[l]), ones)

        cum_h = cum[:, :, :N_HEADS].reshape(bsz, seq, n_pairs, HEADS_PER_STEP)
        o = _attention(q, k, v, cum_h.transpose(0, 2, 1, 3))

        flat = lambda a: a.reshape(bsz * seq, -1)
        x2d = _mixer_out(x2d, flat(o), flat(p), flat(ga), bf16(w_attn_out[l]), bf16(w_o[l]),
                         row(g_ffn2[l]), bf16(w_up2[l][:, :D_FF]), bf16(w_up2[l][:, D_FF:]),
                         bf16(w_down2[l]))
    return x2d.reshape(bsz, seq, D_MODEL)
```

```python
import jax
import jax.numpy as jnp
import numpy as np
from jax import lax
from jax.experimental import pallas as pl
from jax.experimental.pallas import tpu as pltpu

D_MODEL = 1024
D_RNN = 1024
N_RNN_BLOCKS = 8
RNN_BLOCK = D_RNN // N_RNN_BLOCKS
CONV_WIDTH = 4
LRU_C = 8.0
N_HEADS = 16
HEAD_DIM = 64
D_ATTN = N_HEADS * HEAD_DIM
D_FF = 2816
EPS = 1e-6
OFF_F = 2 * D_RNN + 3 * D_ATTN
OFF_GATE = OFF_F + N_HEADS

LANES = 128
SUBLANES = 8
MXU_DIM = 256
VMEM_LIMIT = 56 * 1024 * 1024

FFN_TM = 512
FFN_FC = 256
MIX_TM = 512
ATT_T = 512
HEADS_PER_STEP = LANES // HEAD_DIM
N_SPLIT = 3
AUG_W = 2 * N_SPLIT


def _dot(a, b):
    return jnp.dot(a, b, preferred_element_type=jnp.float32)


def _rms_norm(x, g):
    ms = jnp.mean(x * x, axis=-1, keepdims=True)
    return x * lax.rsqrt(ms + EPS) * g


def _split_bf16(x):
    parts = []
    for _ in range(N_SPLIT):
        t = x.astype(jnp.bfloat16).astype(jnp.float32)
        parts.append(t)
        x = x - t
    return parts


def _aug_base(pair, sub):
    return AUG_W * pair + (HEAD_DIM if sub == 0 else 0)


def _aug_layout():
    place = np.zeros((N_SPLIT, LANES, 2 * LANES), np.float32)
    const = np.zeros((1, 2 * LANES), np.float32)
    for h in range(N_HEADS):
        base = _aug_base(h // HEADS_PER_STEP, h % HEADS_PER_STEP)
        for j in range(N_SPLIT):
            place[j, h, base + j] = 1.0
            const[0, base + N_SPLIT + j] = 1.0
            const[0, LANES + base + j] = 1.0
            place[j, h, LANES + base + N_SPLIT + j] = -1.0
    return jnp.asarray(place, jnp.bfloat16), jnp.asarray(const)


def _const_spec(shape):
    nd = len(shape)
    return pl.BlockSpec(shape, lambda *_: (0,) * nd, pipeline_mode=pl.Buffered(1))


def _swiglu(h, wg_ref, wu_ref, wd_ref):
    acc = jnp.zeros((h.shape[0], D_MODEL), jnp.float32)
    for c in range(D_FF // FFN_FC):
        sl = slice(c * FFN_FC, (c + 1) * FFN_FC)
        gate = _dot(h, wg_ref[:, sl])
        up = _dot(h, wu_ref[:, sl])
        act = (gate * jax.nn.sigmoid(gate) * up).astype(jnp.bfloat16)
        acc = acc + _dot(act, wd_ref[sl, :])
    return acc


def _ffn_kernel(x_ref, g_ref, wg_ref, wu_ref, wd_ref, o_ref):
    x = x_ref[...]
    h = _rms_norm(x, g_ref[...]).astype(jnp.bfloat16)
    o_ref[...] = x + 0.5 * _swiglu(h, wg_ref, wu_ref, wd_ref)


def _ffn(x2d, g, wg, wu, wd):
    n = x2d.shape[0]
    row = pl.BlockSpec((FFN_TM, D_MODEL), lambda i: (i, 0))
    return pl.pallas_call(
        _ffn_kernel,
        grid=(n // FFN_TM,),
        in_specs=[row, _const_spec(g.shape), _const_spec(wg.shape),
                  _const_spec(wu.shape), _const_spec(wd.shape)],
        out_specs=row,
        out_shape=jax.ShapeDtypeStruct(x2d.shape, jnp.float32),
        compiler_params=pltpu.CompilerParams(
            dimension_semantics=("parallel",), vmem_limit_bytes=VMEM_LIMIT),
        name="ffn",
    )(x2d, g, wg, wu, wd)


def _shift_rows(x, d, fill):
    rows = lax.broadcasted_iota(jnp.int32, x.shape, 0)
    return jnp.where(rows >= d, pltpu.roll(x, d, axis=0), fill)


def _linear_scan(a, b, h0):
    tm, c = a.shape
    groups = tm // SUBLANES
    a3 = a.reshape(groups, SUBLANES, c)
    b3 = b.reshape(groups, SUBLANES, c)
    r = lax.broadcasted_iota(jnp.int32, a3.shape, 1)
    d = 1
    while d < SUBLANES:
        keep = r >= d
        a_s = jnp.where(keep, pltpu.roll(a3, d, axis=1), 1.0)
        b_s = jnp.where(keep, pltpu.roll(b3, d, axis=1), 0.0)
        b3 = a3 * b_s + b3
        a3 = a3 * a_s
        d *= 2
    carry = h0
    out = []
    for i in range(groups):
        hg = a3[i] * carry + b3[i]
        out.append(hg)
        carry = hg[SUBLANES - 1:SUBLANES, :]
    return jnp.concatenate(out, axis=0), carry


def _group_mean_sq(x, ones_ref):
    sq = x * x
    hi = sq.astype(jnp.bfloat16)
    lo = (sq - hi.astype(jnp.float32)).astype(jnp.bfloat16)
    outs = []
    for c in range(x.shape[1] // MXU_DIM):
        sl = slice(c * MXU_DIM, (c + 1) * MXU_DIM)
        outs.append(_dot(hi[:, sl], ones_ref[...]) + _dot(lo[:, sl], ones_ref[...]))
    return jnp.concatenate(outs, axis=1)


def _mixer_in_kernel(x_ref, g_ref, wmain_ref, wf_ref, wgate_ref, bgate_ref,
                     convw_ref, convb_ref, wax_ref, ba_ref, bx_ref, lam_ref,
                     gq_ref, gk_ref, bf_ref, wlru_ref, ones_ref, place_ref, augc_ref,
                     q_ref, k_ref, v_ref, augq_ref, augk_ref, p_ref, ga_ref,
                     conv_tail, h_carry, cum_carry):
    tm = x_ref.shape[1]

    @pl.when(pl.program_id(1) == 0)
    def _():
        conv_tail[...] = jnp.zeros_like(conv_tail)
        h_carry[...] = jnp.zeros_like(h_carry)
        cum_carry[...] = jnp.zeros_like(cum_carry)

    x = x_ref[0]
    h = _rms_norm(x, g_ref[...]).astype(jnp.bfloat16)

    z_lx = _dot(h, wmain_ref[:, 0:D_RNN])
    tail = conv_tail[...]
    rows8 = lax.broadcasted_iota(jnp.int32, tail.shape, 0)
    xr = z_lx * convw_ref[CONV_WIDTH - 1:CONV_WIDTH, :] + convb_ref[...]
    for j in range(1, CONV_WIDTH):
        rolled = pltpu.roll(z_lx, j, axis=0)
        head = jnp.where(rows8 < j, pltpu.roll(tail, j, axis=0), rolled[:SUBLANES])
        shifted = jnp.concatenate([head, rolled[SUBLANES:]], axis=0)
        xr = xr + shifted * convw_ref[CONV_WIDTH - 1 - j:CONV_WIDTH - j, :]
    conv_tail[...] = z_lx[tm - SUBLANES:, :]

    xr_b = xr.astype(jnp.bfloat16)
    r_parts, i_parts = [], []
    for n in range(N_RNN_BLOCKS):
        ri = _dot(xr_b[:, n * RNN_BLOCK:(n + 1) * RNN_BLOCK], wax_ref[n])
        r_parts.append(ri[:, :RNN_BLOCK])
        i_parts.append(ri[:, RNN_BLOCK:])
    r = jax.nn.sigmoid(jnp.concatenate(r_parts, axis=1) + ba_ref[...])
    i_gate = jax.nn.sigmoid(jnp.concatenate(i_parts, axis=1) + bx_ref[...])
    neg_lam = -lam_ref[...]
    softplus = jnp.maximum(neg_lam, 0.0) + jnp.log1p(jnp.exp(-jnp.abs(neg_lam)))
    log_a = -LRU_C * r * softplus
    a = jnp.exp(log_a)
    th = jnp.tanh(log_a)
    b = jnp.sqrt(-2.0 * th / (1.0 - th)) * (i_gate * xr)
    hr, h_last = _linear_scan(a, b, h_carry[...])
    h_carry[...] = h_last

    z_lg = _dot(h, wmain_ref[:, D_RNN:2 * D_RNN])
    y = (jax.nn.gelu(z_lg) * hr).astype(jnp.bfloat16)
    y_lru = _dot(y, wlru_ref[...])

    gates = jax.nn.sigmoid(_dot(h, wgate_ref[...]) + bgate_ref[...])
    p_ref[0] = gates[:, :D_MODEL] * y_lru
    ga_ref[0] = gates[:, D_MODEL:]

    off = 2 * D_RNN
    q = _dot(h, wmain_ref[:, off:off + D_ATTN])
    q_ref[0] = (q * lax.rsqrt(_group_mean_sq(q, ones_ref) + EPS) * gq_ref[...]).astype(jnp.bfloat16)
    k = _dot(h, wmain_ref[:, off + D_ATTN:off + 2 * D_ATTN])
    k_ref[0] = (k * lax.rsqrt(_group_mean_sq(k, ones_ref) + EPS) * gk_ref[...]).astype(jnp.bfloat16)
    v_ref[0] = _dot(h, wmain_ref[:, off + 2 * D_ATTN:off + 3 * D_ATTN]).astype(jnp.bfloat16)

    f = _dot(h, wf_ref[...]) + bf_ref[...]
    c = -(jnp.maximum(-f, 0.0) + jnp.log1p(jnp.exp(-jnp.abs(f))))
    d = 1
    while d < tm:
        c = c + _shift_rows(c, d, 0.0)
        d *= 2
    c = c + cum_carry[...]
    cum_carry[...] = c[tm - 1:tm, :]
    aug = augc_ref[...]
    for j, part in enumerate(_split_bf16(c)):
        aug = aug + _dot(part.astype(jnp.bfloat16), place_ref[j])
    augq_ref[0] = aug[:, :LANES].astype(jnp.bfloat16)
    augk_ref[0] = aug[:, LANES:].astype(jnp.bfloat16)


def _mixer_in(x, g, wmain, wf, wgate, bgate, convw, convb, wax, ba, bx, lam,
              gq, gk, bf, wlru, ones, place, augc):
    bsz, seq, _ = x.shape
    tile = lambda w: pl.BlockSpec((1, MIX_TM, w), lambda b, t: (b, t, 0))
    consts = (g, wmain, wf, wgate, bgate, convw, convb, wax, ba, bx, lam, gq, gk, bf, wlru, ones,
              place, augc)
    act = lambda w, dt: jax.ShapeDtypeStruct((bsz, seq, w), dt)
    return pl.pallas_call(
        _mixer_in_kernel,
        grid=(bsz, seq // MIX_TM),
        in_specs=[tile(D_MODEL)] + [_const_spec(c.shape) for c in consts],
        out_specs=[tile(D_ATTN), tile(D_ATTN), tile(D_ATTN), tile(LANES), tile(LANES),
                   tile(D_MODEL), tile(D_MODEL)],
        out_shape=[act(D_ATTN, jnp.bfloat16), act(D_ATTN, jnp.bfloat16),
                   act(D_ATTN, jnp.bfloat16), act(LANES, jnp.bfloat16), act(LANES, jnp.bfloat16),
                   act(D_MODEL, jnp.float32), act(D_MODEL, jnp.float32)],
        scratch_shapes=[pltpu.VMEM((SUBLANES, D_RNN), jnp.float32),
                        pltpu.VMEM((1, D_RNN), jnp.float32),
                        pltpu.VMEM((1, LANES), jnp.float32)],
        compiler_params=pltpu.CompilerParams(
            dimension_semantics=("parallel", "arbitrary"), vmem_limit_bytes=VMEM_LIMIT),
        name="mixer_in",
    )(x, *consts)


def _attn_kernel(q_ref, k_ref, v_ref, aq_ref, ak_ref, o_ref):
    pair = pl.program_id(1)
    qi = pl.program_id(2)
    t = ATT_T
    seq = k_ref.shape[1]

    lane = lax.broadcasted_iota(jnp.int32, (1, LANES), 1)
    row_mask = lambda cond, val=1.0: jnp.where(cond, val, 0.0).astype(jnp.bfloat16)
    own, bias, den = [], [], []
    for h in range(HEADS_PER_STEP):
        base = _aug_base(pair, h)
        own.append((lane < HEAD_DIM) if h == 0 else (lane >= HEAD_DIM))
        bias.append(row_mask((lane >= base) & (lane < base + AUG_W)))
        den.append(row_mask(lane == (HEAD_DIM if h == 0 else 0)))
    q = q_ref[0]
    aq = aq_ref[0]
    qa = [q * row_mask(own[h], HEAD_DIM ** -0.5) + aq * bias[h] for h in range(HEADS_PER_STEP)]
    own = [row_mask(m) for m in own]
    causal = (lax.broadcasted_iota(jnp.int32, (t, t), 0) >= lax.broadcasted_iota(jnp.int32, (t, t), 1))

    def step(kj, carry, masked):
        rows = pl.ds(kj * t, t)
        k = k_ref[0, rows, :]
        v = v_ref[0, rows, :]
        ak = ak_ref[0, rows, :]
        scores = []
        for h in range(HEADS_PER_STEP):
            ka = k * own[h] + ak * bias[h]
            s = lax.dot_general(qa[h], ka, (((1,), (1,)), ((), ())),
                                preferred_element_type=jnp.float32)
            scores.append(jnp.where(causal, s, -jnp.inf) if masked else s)
        probs, maxes, alphas = [], [], []
        for h in range(HEADS_PER_STEP):
            m = carry[h][0]
            m_new = jnp.maximum(m, jnp.max(scores[h], axis=-1, keepdims=True))
            probs.append(jnp.exp(scores[h] - m_new).astype(jnp.bfloat16))
            alphas.append(jnp.exp(m - m_new))
            maxes.append(m_new)
        return tuple((maxes[h], alphas[h] * carry[h][1] + _dot(probs[h], v * own[h] + den[h]))
                     for h in range(HEADS_PER_STEP))

    lane_t = lax.broadcasted_iota(jnp.int32, (t, LANES), 1)
    for n_full in range(seq // t):
        @pl.when(qi == n_full)
        def _(n_full=n_full):
            carry = tuple((jnp.full((t, 1), -jnp.inf, jnp.float32),
                           jnp.zeros((t, LANES), jnp.float32)) for _ in range(HEADS_PER_STEP))
            for kj in range(n_full):
                carry = step(kj, carry, masked=False)
            carry = step(n_full, carry, masked=True)
            o0 = carry[0][1] / carry[0][1][:, HEAD_DIM:HEAD_DIM + 1]
            o1 = carry[1][1] / carry[1][1][:, 0:1]
            o_ref[0] = jnp.where(lane_t < HEAD_DIM, o0, o1).astype(jnp.bfloat16)


def _attention(q, k, v, augq, augk):
    bsz, seq, _ = q.shape
    n_pairs = N_HEADS // HEADS_PER_STEP
    qspec = pl.BlockSpec((1, ATT_T, LANES), lambda b, p, i: (b, i, p))
    kvspec = pl.BlockSpec((1, seq, LANES), lambda b, p, i: (b, 0, p))
    return pl.pallas_call(
        _attn_kernel,
        grid=(bsz, n_pairs, seq // ATT_T),
        in_specs=[qspec, kvspec, kvspec,
                  pl.BlockSpec((1, ATT_T, LANES), lambda b, p, i: (b, i, 0)),
                  pl.BlockSpec((1, seq, LANES), lambda b, p, i: (b, 0, 0))],
        out_specs=qspec,
        out_shape=jax.ShapeDtypeStruct(q.shape, jnp.bfloat16),
        compiler_params=pltpu.CompilerParams(
            dimension_semantics=("parallel", "parallel", "parallel"),
            vmem_limit_bytes=VMEM_LIMIT),
        name="attention",
    )(q, k, v, augq, augk)


def _mixer_out_kernel(x_ref, o_ref, p_ref, ga_ref, wattn_ref, wo_ref,
                      g_ref, wg_ref, wu_ref, wd_ref, out_ref):
    y_attn = _dot(o_ref[...], wattn_ref[...])
    m = (p_ref[...] + ga_ref[...] * y_attn).astype(jnp.bfloat16)
    x = x_ref[...] + _dot(m, wo_ref[...])
    h = _rms_norm(x, g_ref[...]).astype(jnp.bfloat16)
    out_ref[...] = x + 0.5 * _swiglu(h, wg_ref, wu_ref, wd_ref)


def _mixer_out(x2d, o2d, p2d, ga2d, wattn, wo, g, wg, wu, wd):
    n = x2d.shape[0]
    row = pl.BlockSpec((FFN_TM, D_MODEL), lambda i: (i, 0))
    consts = (wattn, wo, g, wg, wu, wd)
    return pl.pallas_call(
        _mixer_out_kernel,
        grid=(n // FFN_TM,),
        in_specs=[row, row, row, row] + [_const_spec(c.shape) for c in consts],
        out_specs=row,
        out_shape=jax.ShapeDtypeStruct(x2d.shape, jnp.float32),
        compiler_params=pltpu.CompilerParams(
            dimension_semantics=("parallel",), vmem_limit_bytes=VMEM_LIMIT),
        name="mixer_out",
    )(x2d, o2d, p2d, ga2d, *consts)


def kernel(x, g_ffn1, w_up1, w_down1, g_mix, w_in, b_gate, conv_w, conv_b, w_a, b_a, w_x, b_x, lam, g_q, g_k, b_forget, w_lru_out, w_attn_out, w_o, g_ffn2, w_up2, w_down2):
    bsz, seq, _ = x.shape
    depth = g_ffn1.shape[0]
    bf16 = lambda w: w.astype(jnp.bfloat16)
    row = lambda p: p.reshape(1, -1)
    gid = jnp.arange(MXU_DIM) // HEAD_DIM
    ones = bf16(jnp.where(gid[:, None] == gid[None, :], 1.0 / HEAD_DIM, 0.0))
    place, augc = _aug_layout()

    x2d = x.reshape(bsz * seq, D_MODEL)
    for l in range(depth):
        x2d = _ffn(x2d, row(g_ffn1[l]), bf16(w_up1[l][:, :D_FF]), bf16(w_up1[l][:, D_FF:]),
                   bf16(w_down1[l]))

        wf = jnp.pad(w_in[l][:, OFF_F:OFF_GATE], ((0, 0), (0, LANES - N_HEADS)))
        bf = jnp.pad(b_forget[l], (0, LANES - N_HEADS))
        wax = jnp.concatenate([w_a[l], w_x[l]], axis=-1)
        q, k, v, augq, augk, p, ga = _mixer_in(
            x2d.reshape(bsz, seq, D_MODEL), row(g_mix[l]), bf16(w_in[l][:, :OFF_F]), bf16(wf),
            bf16(w_in[l][:, OFF_GATE:]), row(b_gate[l]), conv_w[l], row(conv_b[l]), bf16(wax),
            row(b_a[l]), row(b_x[l]), row(lam[l]), row(jnp.tile(g_q[l], N_HEADS)),
            row(jnp.tile(g_k[l], N_HEADS)), row(bf), bf16(w_lru_out[l]), ones, place, augc)

        o = _attention(q, k, v, augq, augk)

        flat = lambda a: a.reshape(bsz * seq, -1)
        x2d = _mixer_out(x2d, flat(o), flat(p), flat(ga), bf16(w_attn_out[l]), bf16(w_o[l]),
                         row(g_ffn2[l]), bf16(w_up2[l][:, :D_FF]), bf16(w_up2[l][:, D_FF:]),
                         bf16(w_down2[l]))
    return x2d.reshape(bsz, seq, D_MODEL)
```

```python
import jax
import jax.numpy as jnp
import numpy as np
from jax import lax
from jax.experimental import pallas as pl
from jax.experimental.pallas import tpu as pltpu

D_MODEL = 1024
D_RNN = 1024
N_RNN_BLOCKS = 8
RNN_BLOCK = D_RNN // N_RNN_BLOCKS
CONV_WIDTH = 4
LRU_C = 8.0
N_HEADS = 16
HEAD_DIM = 64
D_ATTN = N_HEADS * HEAD_DIM
D_FF = 2816
EPS = 1e-6
OFF_F = 2 * D_RNN + 3 * D_ATTN
OFF_GATE = OFF_F + N_HEADS

LANES = 128
SUBLANES = 8
MXU_DIM = 256
VMEM_LIMIT = 56 * 1024 * 1024

FFN_TM = 512
FFN_FC = 256
MIX_TM = 512
ATT_T = 512
HEADS_PER_STEP = LANES // HEAD_DIM
PAIRS_PER_STEP = 2
N_SPLIT = 3
AUG_W = 2 * N_SPLIT


def _dot(a, b):
    return jnp.dot(a, b, preferred_element_type=jnp.float32)


def _rms_norm(x, g):
    ms = jnp.mean(x * x, axis=-1, keepdims=True)
    return x * lax.rsqrt(ms + EPS) * g


def _split_bf16(x):
    parts = []
    for _ in range(N_SPLIT):
        t = x.astype(jnp.bfloat16).astype(jnp.float32)
        parts.append(t)
        x = x - t
    return parts


def _aug_base(pair, sub):
    return AUG_W * pair + (HEAD_DIM if sub == 0 else 0)


def _aug_layout():
    place = np.zeros((N_SPLIT, LANES, 2 * LANES), np.float32)
    const = np.zeros((1, 2 * LANES), np.float32)
    for h in range(N_HEADS):
        base = _aug_base(h // HEADS_PER_STEP, h % HEADS_PER_STEP)
        for j in range(N_SPLIT):
            place[j, h, base + j] = 1.0
            const[0, base + N_SPLIT + j] = 1.0
            const[0, LANES + base + j] = 1.0
            place[j, h, LANES + base + N_SPLIT + j] = -1.0
    return jnp.asarray(place, jnp.bfloat16), jnp.asarray(const)


def _const_spec(shape):
    nd = len(shape)
    return pl.BlockSpec(shape, lambda *_: (0,) * nd, pipeline_mode=pl.Buffered(1))


def _swiglu(h, wg_ref, wu_ref, wd_ref):
    acc = jnp.zeros((h.shape[0], D_MODEL), jnp.float32)
    for c in range(D_FF // FFN_FC):
        sl = slice(c * FFN_FC, (c + 1) * FFN_FC)
        gate = _dot(h, wg_ref[:, sl])
        up = _dot(h, wu_ref[:, sl])
        act = (gate * jax.nn.sigmoid(gate) * up).astype(jnp.bfloat16)
        acc = acc + _dot(act, wd_ref[sl, :])
    return acc


def _ffn_kernel(x_ref, g_ref, wg_ref, wu_ref, wd_ref, o_ref):
    x = x_ref[...]
    h = _rms_norm(x, g_ref[...]).astype(jnp.bfloat16)
    o_ref[...] = x + 0.5 * _swiglu(h, wg_ref, wu_ref, wd_ref)


def _ffn(x2d, g, wg, wu, wd):
    n = x2d.shape[0]
    row = pl.BlockSpec((FFN_TM, D_MODEL), lambda i: (i, 0))
    return pl.pallas_call(
        _ffn_kernel,
        grid=(n // FFN_TM,),
        in_specs=[row, _const_spec(g.shape), _const_spec(wg.shape),
                  _const_spec(wu.shape), _const_spec(wd.shape)],
        out_specs=row,
        out_shape=jax.ShapeDtypeStruct(x2d.shape, jnp.float32),
        compiler_params=pltpu.CompilerParams(
            dimension_semantics=("parallel",), vmem_limit_bytes=VMEM_LIMIT),
        name="ffn",
    )(x2d, g, wg, wu, wd)


def _shift_rows(x, d, fill):
    rows = lax.broadcasted_iota(jnp.int32, x.shape, 0)
    return jnp.where(rows >= d, pltpu.roll(x, d, axis=0), fill)


def _linear_scan(a, b, h0):
    tm, c = a.shape
    groups = tm // SUBLANES
    a3 = a.reshape(groups, SUBLANES, c)
    b3 = b.reshape(groups, SUBLANES, c)
    r = lax.broadcasted_iota(jnp.int32, a3.shape, 1)
    d = 1
    while d < SUBLANES:
        keep = r >= d
        a_s = jnp.where(keep, pltpu.roll(a3, d, axis=1), 1.0)
        b_s = jnp.where(keep, pltpu.roll(b3, d, axis=1), 0.0)
        b3 = a3 * b_s + b3
        a3 = a3 * a_s
        d *= 2
    carry = h0
    out = []
    for i in range(groups):
        hg = a3[i] * carry + b3[i]
        out.append(hg)
        carry = hg[SUBLANES - 1:SUBLANES, :]
    return jnp.concatenate(out, axis=0), carry


def _group_mean_sq(x, ones_ref):
    sq = x * x
    hi = sq.astype(jnp.bfloat16)
    lo = (sq - hi.astype(jnp.float32)).astype(jnp.bfloat16)
    outs = []
    for c in range(x.shape[1] // MXU_DIM):
        sl = slice(c * MXU_DIM, (c + 1) * MXU_DIM)
        outs.append(_dot(hi[:, sl], ones_ref[...]) + _dot(lo[:, sl], ones_ref[...]))
    return jnp.concatenate(outs, axis=1)


def _mixer_in_kernel(x_ref, g_ref, wmain_ref, wf_ref, wgate_ref, bgate_ref,
                     convw_ref, convb_ref, wax_ref, ba_ref, bx_ref, lam_ref,
                     gq_ref, gk_ref, bf_ref, wlru_ref, ones_ref, place_ref, augc_ref,
                     q_ref, k_ref, v_ref, augq_ref, augk_ref, p_ref, ga_ref,
                     conv_tail, h_carry, cum_carry):
    tm = x_ref.shape[1]

    @pl.when(pl.program_id(1) == 0)
    def _():
        conv_tail[...] = jnp.zeros_like(conv_tail)
        h_carry[...] = jnp.zeros_like(h_carry)
        cum_carry[...] = jnp.zeros_like(cum_carry)

    x = x_ref[0]
    h = _rms_norm(x, g_ref[...]).astype(jnp.bfloat16)
    off = 2 * D_RNN

    z_lx = _dot(h, wmain_ref[:, 0:D_RNN])
    tail = conv_tail[...]
    rows8 = lax.broadcasted_iota(jnp.int32, tail.shape, 0)
    xr = z_lx * convw_ref[CONV_WIDTH - 1:CONV_WIDTH, :] + convb_ref[...]
    for j in range(1, CONV_WIDTH):
        rolled = pltpu.roll(z_lx, j, axis=0)
        head = jnp.where(rows8 < j, pltpu.roll(tail, j, axis=0), rolled[:SUBLANES])
        shifted = jnp.concatenate([head, rolled[SUBLANES:]], axis=0)
        xr = xr + shifted * convw_ref[CONV_WIDTH - 1 - j:CONV_WIDTH - j, :]
    conv_tail[...] = z_lx[tm - SUBLANES:, :]

    f = _dot(h, wf_ref[...]) + bf_ref[...]
    c = -(jnp.maximum(-f, 0.0) + jnp.log1p(jnp.exp(-jnp.abs(f))))
    d = 1
    while d < tm:
        c = c + _shift_rows(c, d, 0.0)
        d *= 2
    c = c + cum_carry[...]
    cum_carry[...] = c[tm - 1:tm, :]
    aug = augc_ref[...]
    for j, part in enumerate(_split_bf16(c)):
        aug = aug + _dot(part.astype(jnp.bfloat16), place_ref[j])
    augq_ref[0] = aug[:, :LANES].astype(jnp.bfloat16)
    augk_ref[0] = aug[:, LANES:].astype(jnp.bfloat16)

    xr_b = xr.astype(jnp.bfloat16)
    r_parts, i_parts = [], []
    for n in range(N_RNN_BLOCKS):
        ri = _dot(xr_b[:, n * RNN_BLOCK:(n + 1) * RNN_BLOCK], wax_ref[n])
        r_parts.append(ri[:, :RNN_BLOCK])
        i_parts.append(ri[:, RNN_BLOCK:])
    q = _dot(h, wmain_ref[:, off:off + D_ATTN])
    r = jax.nn.sigmoid(jnp.concatenate(r_parts, axis=1) + ba_ref[...])
    i_gate = jax.nn.sigmoid(jnp.concatenate(i_parts, axis=1) + bx_ref[...])
    neg_lam = -lam_ref[...]
    softplus = jnp.maximum(neg_lam, 0.0) + jnp.log1p(jnp.exp(-jnp.abs(neg_lam)))
    log_a = -LRU_C * r * softplus
    a = jnp.exp(log_a)
    th = jnp.tanh(log_a)
    b = jnp.sqrt(-2.0 * th / (1.0 - th)) * (i_gate * xr)
    k = _dot(h, wmain_ref[:, off + D_ATTN:off + 2 * D_ATTN])
    hr, h_last = _linear_scan(a, b, h_carry[...])
    h_carry[...] = h_last
    v_ref[0] = _dot(h, wmain_ref[:, off + 2 * D_ATTN:off + 3 * D_ATTN]).astype(jnp.bfloat16)

    z_lg = _dot(h, wmain_ref[:, D_RNN:2 * D_RNN])
    y = (jax.nn.gelu(z_lg) * hr).astype(jnp.bfloat16)
    y_lru = _dot(y, wlru_ref[...])
    gates_pre = _dot(h, wgate_ref[...])
    q_ref[0] = (q * lax.rsqrt(_group_mean_sq(q, ones_ref) + EPS) * gq_ref[...]).astype(jnp.bfloat16)
    gates = jax.nn.sigmoid(gates_pre + bgate_ref[...])
    ga_ref[0] = gates[:, D_MODEL:]
    k_ref[0] = (k * lax.rsqrt(_group_mean_sq(k, ones_ref) + EPS) * gk_ref[...]).astype(jnp.bfloat16)
    p_ref[0] = gates[:, :D_MODEL] * y_lru


def _mixer_in(x, g, wmain, wf, wgate, bgate, convw, convb, wax, ba, bx, lam,
              gq, gk, bf, wlru, ones, place, augc):
    bsz, seq, _ = x.shape
    tile = lambda w: pl.BlockSpec((1, MIX_TM, w), lambda b, t: (b, t, 0))
    consts = (g, wmain, wf, wgate, bgate, convw, convb, wax, ba, bx, lam, gq, gk, bf, wlru, ones,
              place, augc)
    act = lambda w, dt: jax.ShapeDtypeStruct((bsz, seq, w), dt)
    return pl.pallas_call(
        _mixer_in_kernel,
        grid=(bsz, seq // MIX_TM),
        in_specs=[tile(D_MODEL)] + [_const_spec(c.shape) for c in consts],
        out_specs=[tile(D_ATTN), tile(D_ATTN), tile(D_ATTN), tile(LANES), tile(LANES),
                   tile(D_MODEL), tile(D_MODEL)],
        out_shape=[act(D_ATTN, jnp.bfloat16), act(D_ATTN, jnp.bfloat16),
                   act(D_ATTN, jnp.bfloat16), act(LANES, jnp.bfloat16), act(LANES, jnp.bfloat16),
                   act(D_MODEL, jnp.float32), act(D_MODEL, jnp.float32)],
        scratch_shapes=[pltpu.VMEM((SUBLANES, D_RNN), jnp.float32),
                        pltpu.VMEM((1, D_RNN), jnp.float32),
                        pltpu.VMEM((1, LANES), jnp.float32)],
        compiler_params=pltpu.CompilerParams(
            dimension_semantics=("parallel", "arbitrary"), vmem_limit_bytes=VMEM_LIMIT),
        name="mixer_in",
    )(x, *consts)


def _attn_kernel(q_ref, k_ref, v_ref, aq_ref, ak_ref, o_ref):
    pair0 = pl.program_id(1) * PAIRS_PER_STEP
    qi = pl.program_id(2)
    t = ATT_T
    seq = k_ref.shape[1]
    heads = [(pr, h) for pr in range(PAIRS_PER_STEP) for h in range(HEADS_PER_STEP)]

    lane = lax.broadcasted_iota(jnp.int32, (1, LANES), 1)
    row_mask = lambda cond, val=1.0: jnp.where(cond, val, 0.0).astype(jnp.bfloat16)
    own_lanes = [lane < HEAD_DIM, lane >= HEAD_DIM]
    own = [row_mask(c) for c in own_lanes]
    den = [row_mask(lane == HEAD_DIM), row_mask(lane == 0)]
    bias = []
    for pr, h in heads:
        base = _aug_base(pair0 + pr, h)
        bias.append(row_mask((lane >= base) & (lane < base + AUG_W)))
    aq = aq_ref[0]
    qa = [q_ref[0, :, pr * LANES:(pr + 1) * LANES] * row_mask(own_lanes[h], HEAD_DIM ** -0.5)
          + aq * bias[i] for i, (pr, h) in enumerate(heads)]
    causal = (lax.broadcasted_iota(jnp.int32, (t, t), 0) >= lax.broadcasted_iota(jnp.int32, (t, t), 1))

    def step(kj, carry, masked):
        rows = pl.ds(kj * t, t)
        ak = ak_ref[0, rows, :]
        scores = []
        for i, (pr, h) in enumerate(heads):
            ka = k_ref[0, rows, pr * LANES:(pr + 1) * LANES] * own[h] + ak * bias[i]
            s = lax.dot_general(qa[i], ka, (((1,), (1,)), ((), ())),
                                preferred_element_type=jnp.float32)
            scores.append(jnp.where(causal, s, -jnp.inf) if masked else s)
        probs, maxes, alphas = [], [], []
        for i in range(len(heads)):
            m = carry[i][0]
            m_new = jnp.maximum(m, jnp.max(scores[i], axis=-1, keepdims=True))
            probs.append(jnp.exp(scores[i] - m_new).astype(jnp.bfloat16))
            alphas.append(jnp.exp(m - m_new))
            maxes.append(m_new)
        out = []
        for i, (pr, h) in enumerate(heads):
            va = v_ref[0, rows, pr * LANES:(pr + 1) * LANES] * own[h] + den[h]
            out.append((maxes[i], alphas[i] * carry[i][1] + _dot(probs[i], va)))
        return tuple(out)

    lane_t = lax.broadcasted_iota(jnp.int32, (t, LANES), 1)
    for n_full in range(seq // t):
        @pl.when(qi == n_full)
        def _(n_full=n_full):
            carry = tuple((jnp.full((t, 1), -jnp.inf, jnp.float32),
                           jnp.zeros((t, LANES), jnp.float32)) for _ in heads)
            for kj in range(n_full):
                carry = step(kj, carry, masked=False)
            carry = step(n_full, carry, masked=True)
            for pr in range(PAIRS_PER_STEP):
                a0, a1 = carry[HEADS_PER_STEP * pr][1], carry[HEADS_PER_STEP * pr + 1][1]
                o0 = a0 / a0[:, HEAD_DIM:HEAD_DIM + 1]
                o1 = a1 / a1[:, 0:1]
                o_ref[0, :, pr * LANES:(pr + 1) * LANES] = (
                    jnp.where(lane_t < HEAD_DIM, o0, o1).astype(jnp.bfloat16))


def _attention(q, k, v, augq, augk):
    bsz, seq, _ = q.shape
    n_pairs = N_HEADS // HEADS_PER_STEP
    width = PAIRS_PER_STEP * LANES
    qspec = pl.BlockSpec((1, ATT_T, width), lambda b, p, i: (b, i, p))
    kvspec = pl.BlockSpec((1, seq, width), lambda b, p, i: (b, 0, p))
    return pl.pallas_call(
        _attn_kernel,
        grid=(bsz, n_pairs // PAIRS_PER_STEP, seq // ATT_T),
        in_specs=[qspec, kvspec, kvspec,
                  pl.BlockSpec((1, ATT_T, LANES), lambda b, p, i: (b, i, 0)),
                  pl.BlockSpec((1, seq, LANES), lambda b, p, i: (b, 0, 0))],
        out_specs=qspec,
        out_shape=jax.ShapeDtypeStruct(q.shape, jnp.bfloat16),
        compiler_params=pltpu.CompilerParams(
            dimension_semantics=("parallel", "parallel", "parallel"),
            vmem_limit_bytes=VMEM_LIMIT),
        name="attention",
    )(q, k, v, augq, augk)


def _mixer_out_kernel(x_ref, o_ref, p_ref, ga_ref, wattn_ref, wo_ref,
                      g_ref, wg_ref, wu_ref, wd_ref, out_ref):
    y_attn = _dot(o_ref[...], wattn_ref[...])
    m = (p_ref[...] + ga_ref[...] * y_attn).astype(jnp.bfloat16)
    x = x_ref[...] + _dot(m, wo_ref[...])
    h = _rms_norm(x, g_ref[...]).astype(jnp.bfloat16)
    out_ref[...] = x + 0.5 * _swiglu(h, wg_ref, wu_ref, wd_ref)


def _mixer_out(x2d, o2d, p2d, ga2d, wattn, wo, g, wg, wu, wd):
    n = x2d.shape[0]
    row = pl.BlockSpec((FFN_TM, D_MODEL), lambda i: (i, 0))
    consts = (wattn, wo, g, wg, wu, wd)
    return pl.pallas_call(
        _mixer_out_kernel,
        grid=(n // FFN_TM,),
        in_specs=[row, row, row, row] + [_const_spec(c.shape) for c in consts],
        out_specs=row,
        out_shape=jax.ShapeDtypeStruct(x2d.shape, jnp.float32),
        compiler_params=pltpu.CompilerParams(
            dimension_semantics=("parallel",), vmem_limit_bytes=VMEM_LIMIT),
        name="mixer_out",
    )(x2d, o2d, p2d, ga2d, *consts)


def kernel(x, g_ffn1, w_up1, w_down1, g_mix, w_in, b_gate, conv_w, conv_b, w_a, b_a, w_x, b_x, lam, g_q, g_k, b_forget, w_lru_out, w_attn_out, w_o, g_ffn2, w_up2, w_down2):
    bsz, seq, _ = x.shape
    depth = g_ffn1.shape[0]
    bf16 = lambda w: w.astype(jnp.bfloat16)
    row = lambda p: p.reshape(1, -1)
    gid = jnp.arange(MXU_DIM) // HEAD_DIM
    ones = bf16(jnp.where(gid[:, None] == gid[None, :], 1.0 / HEAD_DIM, 0.0))
    place, augc = _aug_layout()

    x2d = x.reshape(bsz * seq, D_MODEL)
    for l in range(depth):
        x2d = _ffn(x2d, row(g_ffn1[l]), bf16(w_up1[l][:, :D_FF]), bf16(w_up1[l][:, D_FF:]),
                   bf16(w_down1[l]))

        wf = jnp.pad(w_in[l][:, OFF_F:OFF_GATE], ((0, 0), (0, LANES - N_HEADS)))
        bf = jnp.pad(b_forget[l], (0, LANES - N_HEADS))
        wax = jnp.concatenate([w_a[l], w_x[l]], axis=-1)
        q, k, v, augq, augk, p, ga = _mixer_in(
            x2d.reshape(bsz, seq, D_MODEL), row(g_mix[l]), bf16(w_in[l][:, :OFF_F]), bf16(wf),
            bf16(w_in[l][:, OFF_GATE:]), row(b_gate[l]), conv_w[l], row(conv_b[l]), bf16(wax),
            row(b_a[l]), row(b_x[l]), row(lam[l]), row(jnp.tile(g_q[l], N_HEADS)),
            row(jnp.tile(g_k[l], N_HEADS)), row(bf), bf16(w_lru_out[l]), ones, place, augc)

        o = _attention(q, k, v, augq, augk)

        flat = lambda a: a.reshape(bsz * seq, -1)
        x2d = _mixer_out(x2d, flat(o), flat(p), flat(ga), bf16(w_attn_out[l]), bf16(w_o[l]),
                         row(g_ffn2[l]), bf16(w_up2[l][:, :D_FF]), bf16(w_up2[l][:, D_FF:]),
                         bf16(w_down2[l]))
    return x2d.reshape(bsz, seq, D_MODEL)
```

```python
import jax
import jax.numpy as jnp
import numpy as np
from jax import lax
from jax.experimental import pallas as pl
from jax.experimental.pallas import tpu as pltpu

D_MODEL = 1024
D_RNN = 1024
N_RNN_BLOCKS = 8
RNN_BLOCK = D_RNN // N_RNN_BLOCKS
CONV_WIDTH = 4
LRU_C = 8.0
N_HEADS = 16
HEAD_DIM = 64
D_ATTN = N_HEADS * HEAD_DIM
D_FF = 2816
EPS = 1e-6
OFF_F = 2 * D_RNN + 3 * D_ATTN
OFF_GATE = OFF_F + N_HEADS

LANES = 128
SUBLANES = 8
VMEM_LIMIT = 56 * 1024 * 1024

FFN_TM = 512
FFN_FC = 256
MIX_TM = 512
ATT_T = 512
HEADS_PER_STEP = LANES // HEAD_DIM
PAIRS_PER_STEP = 2
N_SPLIT = 3
AUG_W = 2 * N_SPLIT


def _dot(a, b):
    return jnp.dot(a, b, preferred_element_type=jnp.float32)


def _rms_norm(x, g):
    ms = jnp.mean(x * x, axis=-1, keepdims=True)
    return x * lax.rsqrt(ms + EPS) * g


def _split_bf16(x):
    parts = []
    for _ in range(N_SPLIT):
        t = x.astype(jnp.bfloat16).astype(jnp.float32)
        parts.append(t)
        x = x - t
    return parts


def _aug_base(pair, sub):
    return AUG_W * pair + (HEAD_DIM if sub == 0 else 0)


def _aug_layout():
    place = np.zeros((N_SPLIT, LANES, 2 * LANES), np.float32)
    const = np.zeros((1, 2 * LANES), np.float32)
    for h in range(N_HEADS):
        base = _aug_base(h // HEADS_PER_STEP, h % HEADS_PER_STEP)
        for j in range(N_SPLIT):
            place[j, h, base + j] = 1.0
            const[0, base + N_SPLIT + j] = 1.0
            const[0, LANES + base + j] = 1.0
            place[j, h, LANES + base + N_SPLIT + j] = -1.0
    return jnp.asarray(place, jnp.bfloat16), jnp.asarray(const)


def _const_spec(shape):
    nd = len(shape)
    return pl.BlockSpec(shape, lambda *_: (0,) * nd, pipeline_mode=pl.Buffered(1))


def _swiglu(h, wg_ref, wu_ref, wd_ref):
    acc = jnp.zeros((h.shape[0], D_MODEL), jnp.float32)
    for c in range(D_FF // FFN_FC):
        sl = slice(c * FFN_FC, (c + 1) * FFN_FC)
        gate = _dot(h, wg_ref[:, sl])
        up = _dot(h, wu_ref[:, sl])
        act = (gate * jax.nn.sigmoid(gate) * up).astype(jnp.bfloat16)
        acc = acc + _dot(act, wd_ref[sl, :])
    return acc


def _ffn_kernel(x_ref, g_ref, wg_ref, wu_ref, wd_ref, o_ref):
    x = x_ref[...]
    h = _rms_norm(x, g_ref[...]).astype(jnp.bfloat16)
    o_ref[...] = x + 0.5 * _swiglu(h, wg_ref, wu_ref, wd_ref)


def _ffn(x2d, g, wg, wu, wd):
    n = x2d.shape[0]
    row = pl.BlockSpec((FFN_TM, D_MODEL), lambda i: (i, 0))
    return pl.pallas_call(
        _ffn_kernel,
        grid=(n // FFN_TM,),
        in_specs=[row, _const_spec(g.shape), _const_spec(wg.shape),
                  _const_spec(wu.shape), _const_spec(wd.shape)],
        out_specs=row,
        out_shape=jax.ShapeDtypeStruct(x2d.shape, jnp.float32),
        compiler_params=pltpu.CompilerParams(
            dimension_semantics=("parallel",), vmem_limit_bytes=VMEM_LIMIT),
        name="ffn",
    )(x2d, g, wg, wu, wd)


def _linear_scan(a, b, h0):
    tm, c = a.shape
    groups = tm // SUBLANES
    a3 = a.reshape(groups, SUBLANES, c)
    b3 = b.reshape(groups, SUBLANES, c)
    r = lax.broadcasted_iota(jnp.int32, a3.shape, 1)
    d = 1
    while d < SUBLANES:
        keep = r >= d
        a_s = jnp.where(keep, pltpu.roll(a3, d, axis=1), 1.0)
        b_s = jnp.where(keep, pltpu.roll(b3, d, axis=1), 0.0)
        b3 = a3 * b_s + b3
        a3 = a3 * a_s
        d *= 2
    carry = h0
    out = []
    for i in range(groups):
        hg = a3[i] * carry + b3[i]
        out.append(hg)
        carry = hg[SUBLANES - 1:SUBLANES, :]
    return jnp.concatenate(out, axis=0), carry


def _head_rms_scale(x):
    sq = x * x
    low = lax.broadcasted_iota(jnp.int32, (x.shape[0], LANES), 1) < HEAD_DIM
    outs = []
    for c in range(x.shape[1] // LANES):
        t = sq[:, c * LANES:(c + 1) * LANES]
        s0 = jnp.sum(jnp.where(low, t, 0.0), axis=-1, keepdims=True)
        s1 = jnp.sum(jnp.where(low, 0.0, t), axis=-1, keepdims=True)
        r0 = lax.rsqrt(s0 * (1.0 / HEAD_DIM) + EPS)
        r1 = lax.rsqrt(s1 * (1.0 / HEAD_DIM) + EPS)
        outs.append(jnp.where(low, r0, r1))
    return jnp.concatenate(outs, axis=1)


def _mixer_in_kernel(x_ref, g_ref, wmain_ref, wf_ref, wgate_ref, bgate_ref,
                     convw_ref, convb_ref, wax_ref, ba_ref, bx_ref, lam_ref,
                     gq_ref, gk_ref, bf_ref, wlru_ref, place_ref, augc_ref,
                     q_ref, k_ref, v_ref, augq_ref, augk_ref, p_ref, ga_ref,
                     conv_tail, h_carry, cum_carry):
    tm = x_ref.shape[1]

    @pl.when(pl.program_id(1) == 0)
    def _():
        conv_tail[...] = jnp.zeros_like(conv_tail)
        h_carry[...] = jnp.zeros_like(h_carry)
        cum_carry[...] = jnp.zeros_like(cum_carry)

    x = x_ref[0]
    h = _rms_norm(x, g_ref[...]).astype(jnp.bfloat16)
    off = 2 * D_RNN

    z_lx = _dot(h, wmain_ref[:, 0:D_RNN])
    tail = conv_tail[...]
    rows8 = lax.broadcasted_iota(jnp.int32, tail.shape, 0)
    xr = z_lx * convw_ref[CONV_WIDTH - 1:CONV_WIDTH, :] + convb_ref[...]
    for j in range(1, CONV_WIDTH):
        rolled = pltpu.roll(z_lx, j, axis=0)
        head = jnp.where(rows8 < j, pltpu.roll(tail, j, axis=0), rolled[:SUBLANES])
        shifted = jnp.concatenate([head, rolled[SUBLANES:]], axis=0)
        xr = xr + shifted * convw_ref[CONV_WIDTH - 1 - j:CONV_WIDTH - j, :]
    conv_tail[...] = z_lx[tm - SUBLANES:, :]

    f = _dot(h, wf_ref[...]) + bf_ref[...]
    c = -(jnp.maximum(-f, 0.0) + jnp.log1p(jnp.exp(-jnp.abs(f))))
    c, c_last = _linear_scan(jnp.ones_like(c), c, cum_carry[...])
    cum_carry[...] = c_last
    aug = augc_ref[...]
    for j, part in enumerate(_split_bf16(c)):
        aug = aug + _dot(part.astype(jnp.bfloat16), place_ref[j])
    augq_ref[0] = aug[:, :LANES].astype(jnp.bfloat16)
    augk_ref[0] = aug[:, LANES:].astype(jnp.bfloat16)

    xr_b = xr.astype(jnp.bfloat16)
    r_parts, i_parts = [], []
    for n in range(N_RNN_BLOCKS):
        ri = _dot(xr_b[:, n * RNN_BLOCK:(n + 1) * RNN_BLOCK], wax_ref[n])
        r_parts.append(ri[:, :RNN_BLOCK])
        i_parts.append(ri[:, RNN_BLOCK:])
    q = _dot(h, wmain_ref[:, off:off + D_ATTN])
    r = jax.nn.sigmoid(jnp.concatenate(r_parts, axis=1) + ba_ref[...])
    i_gate = jax.nn.sigmoid(jnp.concatenate(i_parts, axis=1) + bx_ref[...])
    neg_lam = -lam_ref[...]
    softplus = jnp.maximum(neg_lam, 0.0) + jnp.log1p(jnp.exp(-jnp.abs(neg_lam)))
    log_a = -LRU_C * r * softplus
    a = jnp.exp(log_a)
    th = jnp.tanh(log_a)
    b = jnp.sqrt(-2.0 * th / (1.0 - th)) * (i_gate * xr)
    k = _dot(h, wmain_ref[:, off + D_ATTN:off + 2 * D_ATTN])
    hr, h_last = _linear_scan(a, b, h_carry[...])
    h_carry[...] = h_last
    v_ref[0] = _dot(h, wmain_ref[:, off + 2 * D_ATTN:off + 3 * D_ATTN]).astype(jnp.bfloat16)

    z_lg = _dot(h, wmain_ref[:, D_RNN:2 * D_RNN])
    y = (jax.nn.gelu(z_lg) * hr).astype(jnp.bfloat16)
    y_lru = _dot(y, wlru_ref[...])
    gates_pre = _dot(h, wgate_ref[...])
    q_ref[0] = (q * _head_rms_scale(q) * gq_ref[...]).astype(jnp.bfloat16)
    gates = jax.nn.sigmoid(gates_pre + bgate_ref[...])
    ga_ref[0] = gates[:, D_MODEL:]
    k_ref[0] = (k * _head_rms_scale(k) * gk_ref[...]).astype(jnp.bfloat16)
    p_ref[0] = gates[:, :D_MODEL] * y_lru


def _mixer_in(x, g, wmain, wf, wgate, bgate, convw, convb, wax, ba, bx, lam,
              gq, gk, bf, wlru, place, augc):
    bsz, seq, _ = x.shape
    tile = lambda w: pl.BlockSpec((1, MIX_TM, w), lambda b, t: (b, t, 0))
    consts = (g, wmain, wf, wgate, bgate, convw, convb, wax, ba, bx, lam, gq, gk, bf, wlru,
              place, augc)
    act = lambda w, dt: jax.ShapeDtypeStruct((bsz, seq, w), dt)
    return pl.pallas_call(
        _mixer_in_kernel,
        grid=(bsz, seq // MIX_TM),
        in_specs=[tile(D_MODEL)] + [_const_spec(c.shape) for c in consts],
        out_specs=[tile(D_ATTN), tile(D_ATTN), tile(D_ATTN), tile(LANES), tile(LANES),
                   tile(D_MODEL), tile(D_MODEL)],
        out_shape=[act(D_ATTN, jnp.bfloat16), act(D_ATTN, jnp.bfloat16),
                   act(D_ATTN, jnp.bfloat16), act(LANES, jnp.bfloat16), act(LANES, jnp.bfloat16),
                   act(D_MODEL, jnp.float32), act(D_MODEL, jnp.float32)],
        scratch_shapes=[pltpu.VMEM((SUBLANES, D_RNN), jnp.float32),
                        pltpu.VMEM((1, D_RNN), jnp.float32),
                        pltpu.VMEM((1, LANES), jnp.float32)],
        compiler_params=pltpu.CompilerParams(
            dimension_semantics=("parallel", "arbitrary"), vmem_limit_bytes=VMEM_LIMIT),
        name="mixer_in",
    )(x, *consts)


def _attn_kernel(q_ref, k_ref, v_ref, aq_ref, ak_ref, o_ref):
    pair0 = pl.program_id(1) * PAIRS_PER_STEP
    qi = pl.program_id(2)
    t = ATT_T
    seq = k_ref.shape[1]
    heads = [(pr, h) for pr in range(PAIRS_PER_STEP) for h in range(HEADS_PER_STEP)]

    lane = lax.broadcasted_iota(jnp.int32, (1, LANES), 1)
    row_mask = lambda cond: jnp.where(cond, 1.0, 0.0).astype(jnp.bfloat16)
    lane_b = lax.broadcasted_iota(jnp.int32, (t, LANES), 1).astype(jnp.float32).astype(jnp.bfloat16)
    low = lane_b < HEAD_DIM
    own = [low, jnp.logical_not(low)]
    den = [jnp.broadcast_to(row_mask(lane == HEAD_DIM), (t, LANES)),
           jnp.broadcast_to(row_mask(lane == 0), (t, LANES))]
    pair_bias = []
    for pr in range(PAIRS_PER_STEP):
        b0, b1 = _aug_base(pair0 + pr, 0), _aug_base(pair0 + pr, 1)
        pair_bias.append(row_mask(((lane >= b0) & (lane < b0 + AUG_W))
                                  | ((lane >= b1) & (lane < b1 + AUG_W))))
    scale = jnp.bfloat16(HEAD_DIM ** -0.5)
    qa = []
    for pr in range(PAIRS_PER_STEP):
        aq = aq_ref[0] * pair_bias[pr]
        qs = q_ref[0, :, pr * LANES:(pr + 1) * LANES] * scale
        qa += [jnp.where(own[h], qs, aq) for h in range(HEADS_PER_STEP)]
    causal = (lax.broadcasted_iota(jnp.int32, (t, t), 0) >= lax.broadcasted_iota(jnp.int32, (t, t), 1))

    def step(kj, carry, masked):
        rows = pl.ds(kj * t, t)
        aks = [ak_ref[0, rows, :] * pair_bias[pr] for pr in range(PAIRS_PER_STEP)]
        scores = []
        for i, (pr, h) in enumerate(heads):
            ka = jnp.where(own[h], k_ref[0, rows, pr * LANES:(pr + 1) * LANES], aks[pr])
            s = lax.dot_general(qa[i], ka, (((1,), (1,)), ((), ())),
                                preferred_element_type=jnp.float32)
            scores.append(jnp.where(causal, s, -jnp.inf) if masked else s)
        probs, maxes, alphas = [], [], []
        for i in range(len(heads)):
            m = carry[i][0]
            m_new = jnp.maximum(m, jnp.max(scores[i], axis=-1, keepdims=True))
            probs.append(jnp.exp(scores[i] - m_new).astype(jnp.bfloat16))
            alphas.append(jnp.exp(m - m_new))
            maxes.append(m_new)
        out = []
        for i, (pr, h) in enumerate(heads):
            va = jnp.where(own[h], v_ref[0, rows, pr * LANES:(pr + 1) * LANES], den[h])
            out.append((maxes[i], alphas[i] * carry[i][1] + _dot(probs[i], va)))
        return tuple(out)

    lane_t = lax.broadcasted_iota(jnp.int32, (t, LANES), 1)
    for n_full in range(seq // t):
        @pl.when(qi == n_full)
        def _(n_full=n_full):
            carry = tuple((jnp.full((t, 1), -jnp.inf, jnp.float32),
                           jnp.zeros((t, LANES), jnp.float32)) for _ in heads)
            for kj in range(n_full):
                carry = step(kj, carry, masked=False)
            carry = step(n_full, carry, masked=True)
            for pr in range(PAIRS_PER_STEP):
                a0, a1 = carry[HEADS_PER_STEP * pr][1], carry[HEADS_PER_STEP * pr + 1][1]
                o0 = a0 / a0[:, HEAD_DIM:HEAD_DIM + 1]
                o1 = a1 / a1[:, 0:1]
                o_ref[0, :, pr * LANES:(pr + 1) * LANES] = (
                    jnp.where(lane_t < HEAD_DIM, o0, o1).astype(jnp.bfloat16))


def _attention(q, k, v, augq, augk):
    bsz, seq, _ = q.shape
    n_pairs = N_HEADS // HEADS_PER_STEP
    width = PAIRS_PER_STEP * LANES
    qspec = pl.BlockSpec((1, ATT_T, width), lambda b, p, i: (b, i, p))
    kvspec = pl.BlockSpec((1, seq, width), lambda b, p, i: (b, 0, p))
    return pl.pallas_call(
        _attn_kernel,
        grid=(bsz, n_pairs // PAIRS_PER_STEP, seq // ATT_T),
        in_specs=[qspec, kvspec, kvspec,
                  pl.BlockSpec((1, ATT_T, LANES), lambda b, p, i: (b, i, 0)),
                  pl.BlockSpec((1, seq, LANES), lambda b, p, i: (b, 0, 0))],
        out_specs=qspec,
        out_shape=jax.ShapeDtypeStruct(q.shape, jnp.bfloat16),
        compiler_params=pltpu.CompilerParams(
            dimension_semantics=("parallel", "parallel", "parallel"),
            vmem_limit_bytes=VMEM_LIMIT),
        name="attention",
    )(q, k, v, augq, augk)


def _mixer_out_kernel(x_ref, o_ref, p_ref, ga_ref, wattn_ref, wo_ref,
                      g_ref, wg_ref, wu_ref, wd_ref, out_ref):
    y_attn = _dot(o_ref[...], wattn_ref[...])
    m = (p_ref[...] + ga_ref[...] * y_attn).astype(jnp.bfloat16)
    x = x_ref[...] + _dot(m, wo_ref[...])
    h = _rms_norm(x, g_ref[...]).astype(jnp.bfloat16)
    out_ref[...] = x + 0.5 * _swiglu(h, wg_ref, wu_ref, wd_ref)


def _mixer_out(x2d, o2d, p2d, ga2d, wattn, wo, g, wg, wu, wd):
    n = x2d.shape[0]
    row = pl.BlockSpec((FFN_TM, D_MODEL), lambda i: (i, 0))
    consts = (wattn, wo, g, wg, wu, wd)
    return pl.pallas_call(
        _mixer_out_kernel,
        grid=(n // FFN_TM,),
        in_specs=[row, row, row, row] + [_const_spec(c.shape) for c in consts],
        out_specs=row,
        out_shape=jax.ShapeDtypeStruct(x2d.shape, jnp.float32),
        compiler_params=pltpu.CompilerParams(
            dimension_semantics=("parallel",), vmem_limit_bytes=VMEM_LIMIT),
        name="mixer_out",
    )(x2d, o2d, p2d, ga2d, *consts)


def kernel(x, g_ffn1, w_up1, w_down1, g_mix, w_in, b_gate, conv_w, conv_b, w_a, b_a, w_x, b_x, lam, g_q, g_k, b_forget, w_lru_out, w_attn_out, w_o, g_ffn2, w_up2, w_down2):
    bsz, seq, _ = x.shape
    depth = g_ffn1.shape[0]
    bf16 = lambda w: w.astype(jnp.bfloat16)
    row = lambda p: p.reshape(1, -1)
    place, augc = _aug_layout()

    x2d = x.reshape(bsz * seq, D_MODEL)
    for l in range(depth):
        x2d = _ffn(x2d, row(g_ffn1[l]), bf16(w_up1[l][:, :D_FF]), bf16(w_up1[l][:, D_FF:]),
                   bf16(w_down1[l]))

        wf = jnp.pad(w_in[l][:, OFF_F:OFF_GATE], ((0, 0), (0, LANES - N_HEADS)))
        bf = jnp.pad(b_forget[l], (0, LANES - N_HEADS))
        wax = jnp.concatenate([w_a[l], w_x[l]], axis=-1)
        q, k, v, augq, augk, p, ga = _mixer_in(
            x2d.reshape(bsz, seq, D_MODEL), row(g_mix[l]), bf16(w_in[l][:, :OFF_F]), bf16(wf),
            bf16(w_in[l][:, OFF_GATE:]), row(b_gate[l]), conv_w[l], row(conv_b[l]), bf16(wax),
            row(b_a[l]), row(b_x[l]), row(lam[l]), row(jnp.tile(g_q[l], N_HEADS)),
            row(jnp.tile(g_k[l], N_HEADS)), row(bf), bf16(w_lru_out[l]), place, augc)

        o = _attention(q, k, v, augq, augk)

        flat = lambda a: a.reshape(bsz * seq, -1)
        x2d = _mixer_out(x2d, flat(o), flat(p), flat(ga), bf16(w_attn_out[l]), bf16(w_o[l]),
                         row(g_ffn2[l]), bf16(w_up2[l][:, :D_FF]), bf16(w_up2[l][:, D_FF:]),
                         bf16(w_down2[l]))
    return x2d.reshape(bsz, seq, D_MODEL)
```

```python
import jax
import jax.numpy as jnp
import numpy as np
from jax import lax
from jax.experimental import pallas as pl
from jax.experimental.pallas import tpu as pltpu

D_MODEL = 1024
D_RNN = 1024
N_RNN_BLOCKS = 8
RNN_BLOCK = D_RNN // N_RNN_BLOCKS
CONV_WIDTH = 4
LRU_C = 8.0
N_HEADS = 16
HEAD_DIM = 64
D_ATTN = N_HEADS * HEAD_DIM
D_FF = 2816
EPS = 1e-6
OFF_F = 2 * D_RNN + 3 * D_ATTN
OFF_GATE = OFF_F + N_HEADS

LANES = 128
SUBLANES = 8
VMEM_LIMIT = 56 * 1024 * 1024

FFN_TM = 512
FFN_FC = 256
MIX_TM = 512
ATT_T = 512
HEADS_PER_STEP = LANES // HEAD_DIM
PAIRS_PER_STEP = 2
N_SPLIT = 3
AUG_W = 2 * N_SPLIT


def _dot(a, b):
    return jnp.dot(a, b, preferred_element_type=jnp.float32)


def _rms_norm(x, g):
    ms = jnp.mean(x * x, axis=-1, keepdims=True)
    return x * lax.rsqrt(ms + EPS) * g


def _split_bf16(x):
    parts = []
    for _ in range(N_SPLIT):
        t = x.astype(jnp.bfloat16).astype(jnp.float32)
        parts.append(t)
        x = x - t
    return parts


def _aug_base(pair, sub):
    return AUG_W * pair + (HEAD_DIM if sub == 0 else 0)


def _aug_layout():
    place = np.zeros((N_SPLIT, LANES, 2 * LANES), np.float32)
    const = np.zeros((1, 2 * LANES), np.float32)
    for h in range(N_HEADS):
        base = _aug_base(h // HEADS_PER_STEP, h % HEADS_PER_STEP)
        for j in range(N_SPLIT):
            place[j, h, base + j] = 1.0
            const[0, base + N_SPLIT + j] = 1.0
            const[0, LANES + base + j] = 1.0
            place[j, h, LANES + base + N_SPLIT + j] = -1.0
    return jnp.asarray(place, jnp.bfloat16), jnp.asarray(const)


def _const_spec(shape):
    nd = len(shape)
    return pl.BlockSpec(shape, lambda *_: (0,) * nd, pipeline_mode=pl.Buffered(1))


def _swiglu(h, wg_ref, wu_ref, wd_ref):
    acc = jnp.zeros((h.shape[0], D_MODEL), jnp.float32)
    for c in range(D_FF // FFN_FC):
        sl = slice(c * FFN_FC, (c + 1) * FFN_FC)
        gate = _dot(h, wg_ref[:, sl])
        up = _dot(h, wu_ref[:, sl])
        act = (gate * jax.nn.sigmoid(gate) * up).astype(jnp.bfloat16)
        acc = acc + _dot(act, wd_ref[sl, :])
    return acc


def _ffn_kernel(x_ref, g_ref, wg_ref, wu_ref, wd_ref, o_ref):
    x = x_ref[...]
    h = _rms_norm(x, g_ref[...]).astype(jnp.bfloat16)
    o_ref[...] = x + 0.5 * _swiglu(h, wg_ref, wu_ref, wd_ref)


def _ffn(x2d, g, wg, wu, wd):
    n = x2d.shape[0]
    row = pl.BlockSpec((FFN_TM, D_MODEL), lambda i: (i, 0))
    return pl.pallas_call(
        _ffn_kernel,
        grid=(n // FFN_TM,),
        in_specs=[row, _const_spec(g.shape), _const_spec(wg.shape),
                  _const_spec(wu.shape), _const_spec(wd.shape)],
        out_specs=row,
        out_shape=jax.ShapeDtypeStruct(x2d.shape, jnp.float32),
        compiler_params=pltpu.CompilerParams(
            dimension_semantics=("parallel",), vmem_limit_bytes=VMEM_LIMIT),
        name="ffn",
    )(x2d, g, wg, wu, wd)


def _linear_scan(a, b, h0):
    tm, c = a.shape
    groups = tm // SUBLANES
    a3 = a.reshape(groups, SUBLANES, c)
    b3 = b.reshape(groups, SUBLANES, c)
    r = lax.broadcasted_iota(jnp.int32, a3.shape, 1)
    d = 1
    while d < SUBLANES:
        keep = r >= d
        a_s = jnp.where(keep, pltpu.roll(a3, d, axis=1), 1.0)
        b_s = jnp.where(keep, pltpu.roll(b3, d, axis=1), 0.0)
        b3 = a3 * b_s + b3
        a3 = a3 * a_s
        d *= 2
    carry = h0
    out = []
    for i in range(groups):
        hg = a3[i] * carry + b3[i]
        out.append(hg)
        carry = hg[SUBLANES - 1:SUBLANES, :]
    return jnp.concatenate(out, axis=0), carry


def _head_rms_scale(x):
    sq = x * x
    low = lax.broadcasted_iota(jnp.int32, (x.shape[0], LANES), 1) < HEAD_DIM
    outs = []
    for c in range(x.shape[1] // LANES):
        t = sq[:, c * LANES:(c + 1) * LANES]
        s0 = jnp.sum(jnp.where(low, t, 0.0), axis=-1, keepdims=True)
        s1 = jnp.sum(jnp.where(low, 0.0, t), axis=-1, keepdims=True)
        r0 = lax.rsqrt(s0 * (1.0 / HEAD_DIM) + EPS)
        r1 = lax.rsqrt(s1 * (1.0 / HEAD_DIM) + EPS)
        outs.append(jnp.where(low, r0, r1))
    return jnp.concatenate(outs, axis=1)


def _mixer_in_kernel(x_ref, g_ref, wmain_ref, wf_ref, wgate_ref, bgate_ref,
                     convw_ref, convb_ref, wax_ref, ba_ref, bx_ref, lam_ref,
                     gq_ref, gk_ref, bf_ref, wlru_ref, place_ref, augc_ref,
                     q_ref, k_ref, v_ref, augq_ref, augk_ref, p_ref, ga_ref,
                     conv_tail, h_carry, cum_carry):
    tm = x_ref.shape[1]

    @pl.when(pl.program_id(1) == 0)
    def _():
        conv_tail[...] = jnp.zeros_like(conv_tail)
        h_carry[...] = jnp.zeros_like(h_carry)
        cum_carry[...] = jnp.zeros_like(cum_carry)

    x = x_ref[0]
    h = _rms_norm(x, g_ref[...]).astype(jnp.bfloat16)
    off = 2 * D_RNN

    z_lx = _dot(h, wmain_ref[:, 0:D_RNN])
    tail = conv_tail[...]
    rows8 = lax.broadcasted_iota(jnp.int32, tail.shape, 0)
    xr = z_lx * convw_ref[CONV_WIDTH - 1:CONV_WIDTH, :] + convb_ref[...]
    for j in range(1, CONV_WIDTH):
        rolled = pltpu.roll(z_lx, j, axis=0)
        head = jnp.where(rows8 < j, pltpu.roll(tail, j, axis=0), rolled[:SUBLANES])
        shifted = jnp.concatenate([head, rolled[SUBLANES:]], axis=0)
        xr = xr + shifted * convw_ref[CONV_WIDTH - 1 - j:CONV_WIDTH - j, :]
    conv_tail[...] = z_lx[tm - SUBLANES:, :]

    f = _dot(h, wf_ref[...]) + bf_ref[...]
    c = -(jnp.maximum(-f, 0.0) + jnp.log1p(jnp.exp(-jnp.abs(f))))
    c, c_last = _linear_scan(jnp.ones_like(c), c, cum_carry[...])
    cum_carry[...] = c_last
    aug = augc_ref[...]
    for j, part in enumerate(_split_bf16(c)):
        aug = aug + _dot(part.astype(jnp.bfloat16), place_ref[j])
    augq_ref[0] = aug[:, :LANES].astype(jnp.bfloat16)
    augk_ref[0] = aug[:, LANES:].astype(jnp.bfloat16)

    xr_b = xr.astype(jnp.bfloat16)
    r_parts, i_parts = [], []
    for n in range(N_RNN_BLOCKS):
        ri = _dot(xr_b[:, n * RNN_BLOCK:(n + 1) * RNN_BLOCK], wax_ref[n])
        r_parts.append(ri[:, :RNN_BLOCK])
        i_parts.append(ri[:, RNN_BLOCK:])
    q = _dot(h, wmain_ref[:, off:off + D_ATTN])
    r = jax.nn.sigmoid(jnp.concatenate(r_parts, axis=1) + ba_ref[...])
    i_gate = jax.nn.sigmoid(jnp.concatenate(i_parts, axis=1) + bx_ref[...])
    neg_lam = -lam_ref[...]
    softplus = jnp.maximum(neg_lam, 0.0) + jnp.log1p(jnp.exp(-jnp.abs(neg_lam)))
    log_a = -LRU_C * r * softplus
    a = jnp.exp(log_a)
    th = jnp.tanh(log_a)
    b = jnp.sqrt(-2.0 * th / (1.0 - th)) * (i_gate * xr)
    k = _dot(h, wmain_ref[:, off + D_ATTN:off + 2 * D_ATTN])
    hr, h_last = _linear_scan(a, b, h_carry[...])
    h_carry[...] = h_last
    v_ref[0] = _dot(h, wmain_ref[:, off + 2 * D_ATTN:off + 3 * D_ATTN]).astype(jnp.bfloat16)

    z_lg = _dot(h, wmain_ref[:, D_RNN:2 * D_RNN])
    y = (jax.nn.gelu(z_lg) * hr).astype(jnp.bfloat16)
    y_lru = _dot(y, wlru_ref[...])
    gates_pre = _dot(h, wgate_ref[...])
    q_ref[0] = (q * _head_rms_scale(q) * gq_ref[...]).astype(jnp.bfloat16)
    gates = jax.nn.sigmoid(gates_pre + bgate_ref[...])
    ga_ref[0] = gates[:, D_MODEL:]
    k_ref[0] = (k * _head_rms_scale(k) * gk_ref[...]).astype(jnp.bfloat16)
    p_ref[0] = gates[:, :D_MODEL] * y_lru


def _mixer_in(x, g, wmain, wf, wgate, bgate, convw, convb, wax, ba, bx, lam,
              gq, gk, bf, wlru, place, augc):
    bsz, seq, _ = x.shape
    tile = lambda w: pl.BlockSpec((1, MIX_TM, w), lambda b, t: (b, t, 0))
    consts = (g, wmain, wf, wgate, bgate, convw, convb, wax, ba, bx, lam, gq, gk, bf, wlru,
              place, augc)
    act = lambda w, dt: jax.ShapeDtypeStruct((bsz, seq, w), dt)
    return pl.pallas_call(
        _mixer_in_kernel,
        grid=(bsz, seq // MIX_TM),
        in_specs=[tile(D_MODEL)] + [_const_spec(c.shape) for c in consts],
        out_specs=[tile(D_ATTN), tile(D_ATTN), tile(D_ATTN), tile(LANES), tile(LANES),
                   tile(D_MODEL), tile(D_MODEL)],
        out_shape=[act(D_ATTN, jnp.bfloat16), act(D_ATTN, jnp.bfloat16),
                   act(D_ATTN, jnp.bfloat16), act(LANES, jnp.bfloat16), act(LANES, jnp.bfloat16),
                   act(D_MODEL, jnp.float32), act(D_MODEL, jnp.float32)],
        scratch_shapes=[pltpu.VMEM((SUBLANES, D_RNN), jnp.float32),
                        pltpu.VMEM((1, D_RNN), jnp.float32),
                        pltpu.VMEM((1, LANES), jnp.float32)],
        compiler_params=pltpu.CompilerParams(
            dimension_semantics=("parallel", "arbitrary"), vmem_limit_bytes=VMEM_LIMIT),
        name="mixer_in",
    )(x, *consts)


def _attn_kernel(q_ref, k_ref, v_ref, aq_ref, ak_ref, o_ref):
    pair0 = pl.program_id(1) * PAIRS_PER_STEP
    qi = pl.program_id(2)
    t = ATT_T
    seq = k_ref.shape[1]
    heads = [(pr, h) for pr in range(PAIRS_PER_STEP) for h in range(HEADS_PER_STEP)]

    lane = lax.broadcasted_iota(jnp.int32, (1, LANES), 1)
    row_mask = lambda cond: jnp.where(cond, 1.0, 0.0).astype(jnp.bfloat16)
    lane_b = lax.broadcasted_iota(jnp.int32, (t, LANES), 1).astype(jnp.float32).astype(jnp.bfloat16)
    low = lane_b < HEAD_DIM
    own = [low, jnp.logical_not(low)]
    den = [jnp.broadcast_to(row_mask(lane == HEAD_DIM), (t, LANES)),
           jnp.broadcast_to(row_mask(lane == 0), (t, LANES))]
    pair_bias = []
    for pr in range(PAIRS_PER_STEP):
        b0, b1 = _aug_base(pair0 + pr, 0), _aug_base(pair0 + pr, 1)
        pair_bias.append(row_mask(((lane >= b0) & (lane < b0 + AUG_W))
                                  | ((lane >= b1) & (lane < b1 + AUG_W))))
    scale = jnp.bfloat16(HEAD_DIM ** -0.5)
    qa = []
    for pr in range(PAIRS_PER_STEP):
        aq = aq_ref[0] * pair_bias[pr]
        qs = q_ref[0, :, pr * LANES:(pr + 1) * LANES] * scale
        qa += [jnp.where(own[h], qs, aq) for h in range(HEADS_PER_STEP)]
    def qk(q_rows, keys, i, mask):
        pr, h = heads[i]
        ka = jnp.where(mask[h], k_ref[0, keys, pr * LANES:(pr + 1) * LANES],
                       ak_ref[0, keys, :] * pair_bias[pr])
        return lax.dot_general(q_rows, ka, (((1,), (1,)), ((), ())),
                               preferred_element_type=jnp.float32)

    def pv(p, keys, i, mask, den_rows):
        pr, h = heads[i]
        return _dot(p, jnp.where(mask[h], v_ref[0, keys, pr * LANES:(pr + 1) * LANES], den_rows[h]))

    def full_step(kj, carry):
        keys = pl.ds(kj * t, t)
        scores = [qk(qa[i], keys, i, own) for i in range(len(heads))]
        probs, maxes, alphas = [], [], []
        for i in range(len(heads)):
            m = carry[i][0]
            m_new = jnp.maximum(m, jnp.max(scores[i], axis=-1, keepdims=True))
            probs.append(jnp.exp(scores[i] - m_new).astype(jnp.bfloat16))
            alphas.append(jnp.exp(m - m_new))
            maxes.append(m_new)
        return tuple((maxes[i], alphas[i] * carry[i][1] + pv(probs[i], keys, i, own, den))
                     for i in range(len(heads)))

    half = t // 2
    lane_bh = lax.broadcasted_iota(jnp.int32, (half, LANES), 1).astype(jnp.float32).astype(jnp.bfloat16)
    low_h = lane_bh < HEAD_DIM
    own_h = [low_h, jnp.logical_not(low_h)]
    den_h = [d[:half] for d in den]
    tri = (lax.broadcasted_iota(jnp.int32, (half, half), 0)
           >= lax.broadcasted_iota(jnp.int32, (half, half), 1))

    def diag_step(kj, carry):
        keys_l, keys_r = pl.ds(kj * t, half), pl.ds(kj * t + half, half)
        scores = []
        for i in range(len(heads)):
            s_l = qk(qa[i], keys_l, i, own_h)
            s_r = qk(qa[i][half:], keys_r, i, own_h)
            scores.append((jnp.where(tri, s_l[:half], -jnp.inf), s_l[half:],
                           jnp.where(tri, s_r, -jnp.inf)))
        parts = []
        for i in range(len(heads)):
            m = carry[i][0]
            s_lt, s_lb, s_r = scores[i]
            m_top = jnp.maximum(m[:half], jnp.max(s_lt, axis=-1, keepdims=True))
            m_bot = jnp.maximum(m[half:], jnp.maximum(jnp.max(s_lb, axis=-1, keepdims=True),
                                                      jnp.max(s_r, axis=-1, keepdims=True)))
            parts.append((jnp.exp(s_lt - m_top).astype(jnp.bfloat16),
                          jnp.exp(s_lb - m_bot).astype(jnp.bfloat16),
                          jnp.exp(s_r - m_bot).astype(jnp.bfloat16),
                          jnp.exp(m[:half] - m_top), jnp.exp(m[half:] - m_bot)))
        out = []
        for i in range(len(heads)):
            p_lt, p_lb, p_r, a_top, a_bot = parts[i]
            acc = carry[i][1]
            out.append((a_top * acc[:half] + pv(p_lt, keys_l, i, own_h, den_h),
                        a_bot * acc[half:] + pv(p_lb, keys_l, i, own_h, den_h)
                        + pv(p_r, keys_r, i, own_h, den_h)))
        return tuple(out)

    lane_h = lax.broadcasted_iota(jnp.int32, (half, LANES), 1)
    for n_full in range(seq // t):
        @pl.when(qi == n_full)
        def _(n_full=n_full):
            carry = tuple((jnp.full((t, 1), -jnp.inf, jnp.float32),
                           jnp.zeros((t, LANES), jnp.float32)) for _ in heads)
            for kj in range(n_full):
                carry = full_step(kj, carry)
            halves = diag_step(n_full, carry)
            for pr in range(PAIRS_PER_STEP):
                for r, rows_o in enumerate((slice(0, half), slice(half, t))):
                    a0, a1 = halves[HEADS_PER_STEP * pr][r], halves[HEADS_PER_STEP * pr + 1][r]
                    o0 = a0 / a0[:, HEAD_DIM:HEAD_DIM + 1]
                    o1 = a1 / a1[:, 0:1]
                    o_ref[0, rows_o, pr * LANES:(pr + 1) * LANES] = (
                        jnp.where(lane_h < HEAD_DIM, o0, o1).astype(jnp.bfloat16))


def _attention(q, k, v, augq, augk):
    bsz, seq, _ = q.shape
    n_pairs = N_HEADS // HEADS_PER_STEP
    width = PAIRS_PER_STEP * LANES
    qspec = pl.BlockSpec((1, ATT_T, width), lambda b, p, i: (b, i, p))
    kvspec = pl.BlockSpec((1, seq, width), lambda b, p, i: (b, 0, p))
    return pl.pallas_call(
        _attn_kernel,
        grid=(bsz, n_pairs // PAIRS_PER_STEP, seq // ATT_T),
        in_specs=[qspec, kvspec, kvspec,
                  pl.BlockSpec((1, ATT_T, LANES), lambda b, p, i: (b, i, 0)),
                  pl.BlockSpec((1, seq, LANES), lambda b, p, i: (b, 0, 0))],
        out_specs=qspec,
        out_shape=jax.ShapeDtypeStruct(q.shape, jnp.bfloat16),
        compiler_params=pltpu.CompilerParams(
            dimension_semantics=("parallel", "parallel", "parallel"),
            vmem_limit_bytes=VMEM_LIMIT),
        name="attention",
    )(q, k, v, augq, augk)


def _mixer_out_kernel(x_ref, o_ref, p_ref, ga_ref, wattn_ref, wo_ref,
                      g_ref, wg_ref, wu_ref, wd_ref, out_ref):
    y_attn = _dot(o_ref[...], wattn_ref[...])
    m = (p_ref[...] + ga_ref[...] * y_attn).astype(jnp.bfloat16)
    x = x_ref[...] + _dot(m, wo_ref[...])
    h = _rms_norm(x, g_ref[...]).astype(jnp.bfloat16)
    out_ref[...] = x + 0.5 * _swiglu(h, wg_ref, wu_ref, wd_ref)


def _mixer_out(x2d, o2d, p2d, ga2d, wattn, wo, g, wg, wu, wd):
    n = x2d.shape[0]
    row = pl.BlockSpec((FFN_TM, D_MODEL), lambda i: (i, 0))
    consts = (wattn, wo, g, wg, wu, wd)
    return pl.pallas_call(
        _mixer_out_kernel,
        grid=(n // FFN_TM,),
        in_specs=[row, row, row, row] + [_const_spec(c.shape) for c in consts],
        out_specs=row,
        out_shape=jax.ShapeDtypeStruct(x2d.shape, jnp.float32),
        compiler_params=pltpu.CompilerParams(
            dimension_semantics=("parallel",), vmem_limit_bytes=VMEM_LIMIT),
        name="mixer_out",
    )(x2d, o2d, p2d, ga2d, *consts)


def kernel(x, g_ffn1, w_up1, w_down1, g_mix, w_in, b_gate, conv_w, conv_b, w_a, b_a, w_x, b_x, lam, g_q, g_k, b_forget, w_lru_out, w_attn_out, w_o, g_ffn2, w_up2, w_down2):
    bsz, seq, _ = x.shape
    depth = g_ffn1.shape[0]
    bf16 = lambda w: w.astype(jnp.bfloat16)
    row = lambda p: p.reshape(1, -1)
    place, augc = _aug_layout()

    x2d = x.reshape(bsz * seq, D_MODEL)
    for l in range(depth):
        x2d = _ffn(x2d, row(g_ffn1[l]), bf16(w_up1[l][:, :D_FF]), bf16(w_up1[l][:, D_FF:]),
                   bf16(w_down1[l]))

        wf = jnp.pad(w_in[l][:, OFF_F:OFF_GATE], ((0, 0), (0, LANES - N_HEADS)))
        bf = jnp.pad(b_forget[l], (0, LANES - N_HEADS))
        wax = jnp.concatenate([w_a[l], w_x[l]], axis=-1)
        q, k, v, augq, augk, p, ga = _mixer_in(
            x2d.reshape(bsz, seq, D_MODEL), row(g_mix[l]), bf16(w_in[l][:, :OFF_F]), bf16(wf),
            bf16(w_in[l][:, OFF_GATE:]), row(b_gate[l]), conv_w[l], row(conv_b[l]), bf16(wax),
            row(b_a[l]), row(b_x[l]), row(lam[l]), row(jnp.tile(g_q[l], N_HEADS)),
            row(jnp.tile(g_k[l], N_HEADS)), row(bf), bf16(w_lru_out[l]), place, augc)

        o = _attention(q, k, v, augq, augk)

        flat = lambda a: a.reshape(bsz * seq, -1)
        x2d = _mixer_out(x2d, flat(o), flat(p), flat(ga), bf16(w_attn_out[l]), bf16(w_o[l]),
                         row(g_ffn2[l]), bf16(w_up2[l][:, :D_FF]), bf16(w_up2[l][:, D_FF:]),
                         bf16(w_down2[l]))
    return x2d.reshape(bsz, seq, D_MODEL)
```

```python
import jax
import jax.numpy as jnp
import numpy as np
from jax import lax
from jax.experimental import pallas as pl
from jax.experimental.pallas import tpu as pltpu

D_MODEL = 1024
D_RNN = 1024
N_RNN_BLOCKS = 8
RNN_BLOCK = D_RNN // N_RNN_BLOCKS
CONV_WIDTH = 4
LRU_C = 8.0
N_HEADS = 16
HEAD_DIM = 64
D_ATTN = N_HEADS * HEAD_DIM
D_FF = 2816
EPS = 1e-6
OFF_F = 2 * D_RNN + 3 * D_ATTN
OFF_GATE = OFF_F + N_HEADS

LANES = 128
SUBLANES = 8
VMEM_LIMIT = 56 * 1024 * 1024

FFN_TM = 512
FFN_FC = 256
MIX_TM = 512
ATT_T = 512
HEADS_PER_STEP = LANES // HEAD_DIM
PAIRS_PER_STEP = 2
N_SPLIT = 3
AUG_W = 2 * N_SPLIT


def _dot(a, b):
    return jnp.dot(a, b, preferred_element_type=jnp.float32)


def _rms_norm(x, g):
    ms = jnp.mean(x * x, axis=-1, keepdims=True)
    return x * lax.rsqrt(ms + EPS) * g


def _split_bf16(x):
    parts = []
    for _ in range(N_SPLIT):
        t = x.astype(jnp.bfloat16).astype(jnp.float32)
        parts.append(t)
        x = x - t
    return parts


def _aug_base(pair, sub):
    return AUG_W * pair + (HEAD_DIM if sub == 0 else 0)


def _aug_layout():
    place = np.zeros((N_SPLIT, LANES, 2 * LANES), np.float32)
    const = np.zeros((1, 2 * LANES), np.float32)
    for h in range(N_HEADS):
        base = _aug_base(h // HEADS_PER_STEP, h % HEADS_PER_STEP)
        for j in range(N_SPLIT):
            place[j, h, base + j] = 1.0
            const[0, base + N_SPLIT + j] = 1.0
            const[0, LANES + base + j] = 1.0
            place[j, h, LANES + base + N_SPLIT + j] = -1.0
    return jnp.asarray(place, jnp.bfloat16), jnp.asarray(const)


def _layer(arr, l, cols=None, col_block=0):
    block = arr.shape[1:] if cols is None else arr.shape[1:-1] + (cols,)
    return arr, (None,) + block, (l,) + (0,) * (len(block) - 1) + (col_block,)


def _whole(arr):
    return arr, arr.shape, (0,) * arr.ndim


def _const_spec(operand):
    _, block, index = operand
    return pl.BlockSpec(block, lambda *_: index, pipeline_mode=pl.Buffered(1))


def _swiglu(h, wg_ref, wu_ref, wd_ref):
    acc = jnp.zeros((h.shape[0], D_MODEL), jnp.float32)
    for c in range(D_FF // FFN_FC):
        sl = slice(c * FFN_FC, (c + 1) * FFN_FC)
        gate = _dot(h, wg_ref[:, sl])
        up = _dot(h, wu_ref[:, sl])
        act = (gate * jax.nn.sigmoid(gate) * up).astype(jnp.bfloat16)
        acc = acc + _dot(act, wd_ref[sl, :])
    return acc


def _ffn_kernel(x_ref, g_ref, wg_ref, wu_ref, wd_ref, o_ref):
    x = x_ref[...]
    h = _rms_norm(x, g_ref[...]).astype(jnp.bfloat16)
    o_ref[...] = x + 0.5 * _swiglu(h, wg_ref, wu_ref, wd_ref)


def _ffn(x2d, g, wg, wu, wd):
    n = x2d.shape[0]
    row = pl.BlockSpec((FFN_TM, D_MODEL), lambda i: (i, 0))
    return pl.pallas_call(
        _ffn_kernel,
        grid=(n // FFN_TM,),
        in_specs=[row] + [_const_spec(c) for c in (g, wg, wu, wd)],
        out_specs=row,
        out_shape=jax.ShapeDtypeStruct(x2d.shape, jnp.float32),
        compiler_params=pltpu.CompilerParams(
            dimension_semantics=("parallel",), vmem_limit_bytes=VMEM_LIMIT),
        name="ffn",
    )(x2d, *(c[0] for c in (g, wg, wu, wd)))


def _linear_scan(a, b, h0):
    tm, c = a.shape
    groups = tm // SUBLANES
    a3 = a.reshape(groups, SUBLANES, c)
    b3 = b.reshape(groups, SUBLANES, c)
    r = lax.broadcasted_iota(jnp.int32, a3.shape, 1)
    d = 1
    while d < SUBLANES:
        keep = r >= d
        a_s = jnp.where(keep, pltpu.roll(a3, d, axis=1), 1.0)
        b_s = jnp.where(keep, pltpu.roll(b3, d, axis=1), 0.0)
        b3 = a3 * b_s + b3
        a3 = a3 * a_s
        d *= 2
    carry = h0
    out = []
    for i in range(groups):
        hg = a3[i] * carry + b3[i]
        out.append(hg)
        carry = hg[SUBLANES - 1:SUBLANES, :]
    return jnp.concatenate(out, axis=0), carry


def _head_rms_scale(x):
    sq = x * x
    low = lax.broadcasted_iota(jnp.int32, (x.shape[0], LANES), 1) < HEAD_DIM
    outs = []
    for c in range(x.shape[1] // LANES):
        t = sq[:, c * LANES:(c + 1) * LANES]
        s0 = jnp.sum(jnp.where(low, t, 0.0), axis=-1, keepdims=True)
        s1 = jnp.sum(jnp.where(low, 0.0, t), axis=-1, keepdims=True)
        r0 = lax.rsqrt(s0 * (1.0 / HEAD_DIM) + EPS)
        r1 = lax.rsqrt(s1 * (1.0 / HEAD_DIM) + EPS)
        outs.append(jnp.where(low, r0, r1))
    return jnp.concatenate(outs, axis=1)


def _mixer_in_kernel(x_ref, g_ref, wmain_ref, wf_ref, wgate_ref, bgate_ref,
                     convw_ref, convb_ref, wax_ref, ba_ref, bx_ref, lam_ref,
                     gq_ref, gk_ref, bf_ref, wlru_ref, place_ref, augc_ref,
                     q_ref, k_ref, v_ref, augq_ref, augk_ref, p_ref, ga_ref,
                     conv_tail, h_carry, cum_carry):
    tm = x_ref.shape[1]

    @pl.when(pl.program_id(1) == 0)
    def _():
        conv_tail[...] = jnp.zeros_like(conv_tail)
        h_carry[...] = jnp.zeros_like(h_carry)
        cum_carry[...] = jnp.zeros_like(cum_carry)

    x = x_ref[0]
    h = _rms_norm(x, g_ref[...]).astype(jnp.bfloat16)
    off = 2 * D_RNN

    z_lx = _dot(h, wmain_ref[:, 0:D_RNN])
    tail = conv_tail[...]
    rows8 = lax.broadcasted_iota(jnp.int32, tail.shape, 0)
    xr = z_lx * convw_ref[CONV_WIDTH - 1:CONV_WIDTH, :] + convb_ref[...]
    for j in range(1, CONV_WIDTH):
        rolled = pltpu.roll(z_lx, j, axis=0)
        head = jnp.where(rows8 < j, pltpu.roll(tail, j, axis=0), rolled[:SUBLANES])
        shifted = jnp.concatenate([head, rolled[SUBLANES:]], axis=0)
        xr = xr + shifted * convw_ref[CONV_WIDTH - 1 - j:CONV_WIDTH - j, :]
    conv_tail[...] = z_lx[tm - SUBLANES:, :]

    f = _dot(h, wf_ref[...]) + bf_ref[...]
    c = -(jnp.maximum(-f, 0.0) + jnp.log1p(jnp.exp(-jnp.abs(f))))
    c, c_last = _linear_scan(jnp.ones_like(c), c, cum_carry[...])
    cum_carry[...] = c_last
    aug = augc_ref[...]
    for j, part in enumerate(_split_bf16(c)):
        aug = aug + _dot(part.astype(jnp.bfloat16), place_ref[j])
    augq_ref[0] = aug[:, :LANES].astype(jnp.bfloat16)
    augk_ref[0] = aug[:, LANES:].astype(jnp.bfloat16)

    xr_b = xr.astype(jnp.bfloat16)
    r_parts, i_parts = [], []
    for n in range(N_RNN_BLOCKS):
        ri = _dot(xr_b[:, n * RNN_BLOCK:(n + 1) * RNN_BLOCK], wax_ref[n])
        r_parts.append(ri[:, :RNN_BLOCK])
        i_parts.append(ri[:, RNN_BLOCK:])
    q = _dot(h, wmain_ref[:, off:off + D_ATTN])
    r = jax.nn.sigmoid(jnp.concatenate(r_parts, axis=1) + ba_ref[...])
    i_gate = jax.nn.sigmoid(jnp.concatenate(i_parts, axis=1) + bx_ref[...])
    neg_lam = -lam_ref[...]
    softplus = jnp.maximum(neg_lam, 0.0) + jnp.log1p(jnp.exp(-jnp.abs(neg_lam)))
    log_a = -LRU_C * r * softplus
    a = jnp.exp(log_a)
    th = jnp.tanh(log_a)
    b = jnp.sqrt(-2.0 * th / (1.0 - th)) * (i_gate * xr)
    k = _dot(h, wmain_ref[:, off + D_ATTN:off + 2 * D_ATTN])
    hr, h_last = _linear_scan(a, b, h_carry[...])
    h_carry[...] = h_last
    v_ref[0] = _dot(h, wmain_ref[:, off + 2 * D_ATTN:off + 3 * D_ATTN]).astype(jnp.bfloat16)

    z_lg = _dot(h, wmain_ref[:, D_RNN:2 * D_RNN])
    y = (jax.nn.gelu(z_lg) * hr).astype(jnp.bfloat16)
    y_lru = _dot(y, wlru_ref[...])
    gates_pre = _dot(h, wgate_ref[...])
    q_ref[0] = (q * _head_rms_scale(q) * gq_ref[...]).astype(jnp.bfloat16)
    gates = jax.nn.sigmoid(gates_pre + bgate_ref[...])
    ga_ref[0] = gates[:, D_MODEL:]
    k_ref[0] = (k * _head_rms_scale(k) * gk_ref[...]).astype(jnp.bfloat16)
    p_ref[0] = gates[:, :D_MODEL] * y_lru


def _mixer_in(x, g, wmain, wf, wgate, bgate, convw, convb, wax, ba, bx, lam,
              gq, gk, bf, wlru, place, augc):
    bsz, seq, _ = x.shape
    tile = lambda w: pl.BlockSpec((1, MIX_TM, w), lambda b, t: (b, t, 0))
    consts = (g, wmain, wf, wgate, bgate, convw, convb, wax, ba, bx, lam, gq, gk, bf, wlru,
              place, augc)
    act = lambda w, dt: jax.ShapeDtypeStruct((bsz, seq, w), dt)
    return pl.pallas_call(
        _mixer_in_kernel,
        grid=(bsz, seq // MIX_TM),
        in_specs=[tile(D_MODEL)] + [_const_spec(c) for c in consts],
        out_specs=[tile(D_ATTN), tile(D_ATTN), tile(D_ATTN), tile(LANES), tile(LANES),
                   tile(D_MODEL), tile(D_MODEL)],
        out_shape=[act(D_ATTN, jnp.bfloat16), act(D_ATTN, jnp.bfloat16),
                   act(D_ATTN, jnp.bfloat16), act(LANES, jnp.bfloat16), act(LANES, jnp.bfloat16),
                   act(D_MODEL, jnp.float32), act(D_MODEL, jnp.float32)],
        scratch_shapes=[pltpu.VMEM((SUBLANES, D_RNN), jnp.float32),
                        pltpu.VMEM((1, D_RNN), jnp.float32),
                        pltpu.VMEM((1, LANES), jnp.float32)],
        compiler_params=pltpu.CompilerParams(
            dimension_semantics=("parallel", "arbitrary"), vmem_limit_bytes=VMEM_LIMIT),
        name="mixer_in",
    )(x, *(c[0] for c in consts))


def _attn_kernel(q_ref, k_ref, v_ref, aq_ref, ak_ref, o_ref):
    pair0 = pl.program_id(1) * PAIRS_PER_STEP
    qi = pl.program_id(2)
    t = ATT_T
    seq = k_ref.shape[1]
    heads = [(pr, h) for pr in range(PAIRS_PER_STEP) for h in range(HEADS_PER_STEP)]

    lane = lax.broadcasted_iota(jnp.int32, (1, LANES), 1)
    row_mask = lambda cond: jnp.where(cond, 1.0, 0.0).astype(jnp.bfloat16)
    lane_b = lax.broadcasted_iota(jnp.int32, (t, LANES), 1).astype(jnp.float32).astype(jnp.bfloat16)
    low = lane_b < HEAD_DIM
    own = [low, jnp.logical_not(low)]
    den = [jnp.broadcast_to(row_mask(lane == HEAD_DIM), (t, LANES)),
           jnp.broadcast_to(row_mask(lane == 0), (t, LANES))]
    pair_bias = []
    for pr in range(PAIRS_PER_STEP):
        b0, b1 = _aug_base(pair0 + pr, 0), _aug_base(pair0 + pr, 1)
        pair_bias.append(row_mask(((lane >= b0) & (lane < b0 + AUG_W))
                                  | ((lane >= b1) & (lane < b1 + AUG_W))))
    scale = jnp.bfloat16(HEAD_DIM ** -0.5)
    qa = []
    for pr in range(PAIRS_PER_STEP):
        aq = aq_ref[0] * pair_bias[pr]
        qs = q_ref[0, :, pr * LANES:(pr + 1) * LANES] * scale
        qa += [jnp.where(own[h], qs, aq) for h in range(HEADS_PER_STEP)]
    def qk(q_rows, keys, i, mask):
        pr, h = heads[i]
        ka = jnp.where(mask[h], k_ref[0, keys, pr * LANES:(pr + 1) * LANES],
                       ak_ref[0, keys, :] * pair_bias[pr])
        return lax.dot_general(q_rows, ka, (((1,), (1,)), ((), ())),
                               preferred_element_type=jnp.float32)

    def pv(p, keys, i, mask, den_rows):
        pr, h = heads[i]
        return _dot(p, jnp.where(mask[h], v_ref[0, keys, pr * LANES:(pr + 1) * LANES], den_rows[h]))

    def full_step(kj, carry):
        keys = pl.ds(kj * t, t)
        scores = [qk(qa[i], keys, i, own) for i in range(len(heads))]
        probs, maxes, alphas = [], [], []
        for i in range(len(heads)):
            m = carry[i][0]
            m_new = jnp.maximum(m, jnp.max(scores[i], axis=-1, keepdims=True))
            probs.append(jnp.exp(scores[i] - m_new).astype(jnp.bfloat16))
            alphas.append(jnp.exp(m - m_new))
            maxes.append(m_new)
        return tuple((maxes[i], alphas[i] * carry[i][1] + pv(probs[i], keys, i, own, den))
                     for i in range(len(heads)))

    half = t // 2
    lane_bh = lax.broadcasted_iota(jnp.int32, (half, LANES), 1).astype(jnp.float32).astype(jnp.bfloat16)
    low_h = lane_bh < HEAD_DIM
    own_h = [low_h, jnp.logical_not(low_h)]
    den_h = [d[:half] for d in den]
    tri = (lax.broadcasted_iota(jnp.int32, (half, half), 0)
           >= lax.broadcasted_iota(jnp.int32, (half, half), 1))

    def diag_step(kj, carry):
        keys_l, keys_r = pl.ds(kj * t, half), pl.ds(kj * t + half, half)
        scores = []
        for i in range(len(heads)):
            s_l = qk(qa[i], keys_l, i, own_h)
            s_r = qk(qa[i][half:], keys_r, i, own_h)
            scores.append((jnp.where(tri, s_l[:half], -jnp.inf), s_l[half:],
                           jnp.where(tri, s_r, -jnp.inf)))
        parts = []
        for i in range(len(heads)):
            m = carry[i][0]
            s_lt, s_lb, s_r = scores[i]
            m_top = jnp.maximum(m[:half], jnp.max(s_lt, axis=-1, keepdims=True))
            m_bot = jnp.maximum(m[half:], jnp.maximum(jnp.max(s_lb, axis=-1, keepdims=True),
                                                      jnp.max(s_r, axis=-1, keepdims=True)))
            parts.append((jnp.exp(s_lt - m_top).astype(jnp.bfloat16),
                          jnp.exp(s_lb - m_bot).astype(jnp.bfloat16),
                          jnp.exp(s_r - m_bot).astype(jnp.bfloat16),
                          jnp.exp(m[:half] - m_top), jnp.exp(m[half:] - m_bot)))
        out = []
        for i in range(len(heads)):
            p_lt, p_lb, p_r, a_top, a_bot = parts[i]
            acc = carry[i][1]
            out.append((a_top * acc[:half] + pv(p_lt, keys_l, i, own_h, den_h),
                        a_bot * acc[half:] + pv(p_lb, keys_l, i, own_h, den_h)
                        + pv(p_r, keys_r, i, own_h, den_h)))
        return tuple(out)

    lane_h = lax.broadcasted_iota(jnp.int32, (half, LANES), 1)
    for n_full in range(seq // t):
        @pl.when(qi == n_full)
        def _(n_full=n_full):
            carry = tuple((jnp.full((t, 1), -jnp.inf, jnp.float32),
                           jnp.zeros((t, LANES), jnp.float32)) for _ in heads)
            for kj in range(n_full):
                carry = full_step(kj, carry)
            halves = diag_step(n_full, carry)
            for pr in range(PAIRS_PER_STEP):
                for r, rows_o in enumerate((slice(0, half), slice(half, t))):
                    a0, a1 = halves[HEADS_PER_STEP * pr][r], halves[HEADS_PER_STEP * pr + 1][r]
                    o0 = a0 / a0[:, HEAD_DIM:HEAD_DIM + 1]
                    o1 = a1 / a1[:, 0:1]
                    o_ref[0, rows_o, pr * LANES:(pr + 1) * LANES] = (
                        jnp.where(lane_h < HEAD_DIM, o0, o1).astype(jnp.bfloat16))


def _attention(q, k, v, augq, augk):
    bsz, seq, _ = q.shape
    n_pairs = N_HEADS // HEADS_PER_STEP
    width = PAIRS_PER_STEP * LANES
    qspec = pl.BlockSpec((1, ATT_T, width), lambda b, p, i: (b, i, p))
    kvspec = pl.BlockSpec((1, seq, width), lambda b, p, i: (b, 0, p))
    return pl.pallas_call(
        _attn_kernel,
        grid=(bsz, n_pairs // PAIRS_PER_STEP, seq // ATT_T),
        in_specs=[qspec, kvspec, kvspec,
                  pl.BlockSpec((1, ATT_T, LANES), lambda b, p, i: (b, i, 0)),
                  pl.BlockSpec((1, seq, LANES), lambda b, p, i: (b, 0, 0))],
        out_specs=qspec,
        out_shape=jax.ShapeDtypeStruct(q.shape, jnp.bfloat16),
        compiler_params=pltpu.CompilerParams(
            dimension_semantics=("parallel", "parallel", "parallel"),
            vmem_limit_bytes=VMEM_LIMIT),
        name="attention",
    )(q, k, v, augq, augk)


def _mixer_out_kernel(x_ref, o_ref, p_ref, ga_ref, wattn_ref, wo_ref,
                      g_ref, wg_ref, wu_ref, wd_ref, out_ref):
    y_attn = _dot(o_ref[...], wattn_ref[...])
    m = (p_ref[...] + ga_ref[...] * y_attn).astype(jnp.bfloat16)
    x = x_ref[...] + _dot(m, wo_ref[...])
    h = _rms_norm(x, g_ref[...]).astype(jnp.bfloat16)
    out_ref[...] = x + 0.5 * _swiglu(h, wg_ref, wu_ref, wd_ref)


def _mixer_out(x2d, o2d, p2d, ga2d, wattn, wo, g, wg, wu, wd):
    n = x2d.shape[0]
    row = pl.BlockSpec((FFN_TM, D_MODEL), lambda i: (i, 0))
    consts = (wattn, wo, g, wg, wu, wd)
    return pl.pallas_call(
        _mixer_out_kernel,
        grid=(n // FFN_TM,),
        in_specs=[row, row, row, row] + [_const_spec(c) for c in consts],
        out_specs=row,
        out_shape=jax.ShapeDtypeStruct(x2d.shape, jnp.float32),
        compiler_params=pltpu.CompilerParams(
            dimension_semantics=("parallel",), vmem_limit_bytes=VMEM_LIMIT),
        name="mixer_out",
    )(x2d, o2d, p2d, ga2d, *(c[0] for c in consts))


def kernel(x, g_ffn1, w_up1, w_down1, g_mix, w_in, b_gate, conv_w, conv_b, w_a, b_a, w_x, b_x, lam, g_q, g_k, b_forget, w_lru_out, w_attn_out, w_o, g_ffn2, w_up2, w_down2):
    bsz, seq, _ = x.shape
    depth = g_ffn1.shape[0]
    bf16 = lambda w: w.astype(jnp.bfloat16)
    vec = lambda p: p.reshape(p.shape[0], 1, -1)
    wup1, wdn1, wup2, wdn2 = bf16(w_up1), bf16(w_down1), bf16(w_up2), bf16(w_down2)
    win = bf16(w_in)
    wf = bf16(jnp.pad(w_in[:, :, OFF_F:OFF_GATE], ((0, 0), (0, 0), (0, LANES - N_HEADS))))
    wgate = bf16(w_in[:, :, OFF_GATE:])
    bfp = vec(jnp.pad(b_forget, ((0, 0), (0, LANES - N_HEADS))))
    wax = bf16(jnp.concatenate([w_a, w_x], axis=-1))
    gq, gk = vec(jnp.tile(g_q, (1, N_HEADS))), vec(jnp.tile(g_k, (1, N_HEADS)))
    wlru, wattn, wo = bf16(w_lru_out), bf16(w_attn_out), bf16(w_o)
    place, augc = _aug_layout()

    x2d = x.reshape(bsz * seq, D_MODEL)
    for l in range(depth):
        x2d = _ffn(x2d, _layer(vec(g_ffn1), l), _layer(wup1, l, D_FF, 0), _layer(wup1, l, D_FF, 1),
                   _layer(wdn1, l))

        q, k, v, augq, augk, p, ga = _mixer_in(
            x2d.reshape(bsz, seq, D_MODEL), _layer(vec(g_mix), l), _layer(win, l, OFF_F, 0),
            _layer(wf, l), _layer(wgate, l), _layer(vec(b_gate), l), _layer(conv_w, l),
            _layer(vec(conv_b), l), _layer(wax, l), _layer(vec(b_a), l), _layer(vec(b_x), l),
            _layer(vec(lam), l), _layer(gq, l), _layer(gk, l), _layer(bfp, l), _layer(wlru, l),
            _whole(place), _whole(augc))

        o = _attention(q, k, v, augq, augk)

        flat = lambda a: a.reshape(bsz * seq, -1)
        x2d = _mixer_out(x2d, flat(o), flat(p), flat(ga), _layer(wattn, l), _layer(wo, l),
                         _layer(vec(g_ffn2), l), _layer(wup2, l, D_FF, 0), _layer(wup2, l, D_FF, 1),
                         _layer(wdn2, l))
    return x2d.reshape(bsz, seq, D_MODEL)
```

```python
import jax
import jax.numpy as jnp
import numpy as np
from jax import lax
from jax.experimental import pallas as pl
from jax.experimental.pallas import tpu as pltpu

D_MODEL = 1024
D_RNN = 1024
N_RNN_BLOCKS = 8
RNN_BLOCK = D_RNN // N_RNN_BLOCKS
CONV_WIDTH = 4
LRU_C = 8.0
N_HEADS = 16
HEAD_DIM = 64
D_ATTN = N_HEADS * HEAD_DIM
D_FF = 2816
EPS = 1e-6
OFF_F = 2 * D_RNN + 3 * D_ATTN
OFF_GATE = OFF_F + N_HEADS

LANES = 128
SUBLANES = 8
VMEM_LIMIT = 56 * 1024 * 1024

FFN_TM = 512
FFN_FC = 256
MIX_TM = 512
ATT_T = 512
HEADS_PER_STEP = LANES // HEAD_DIM
PAIRS_PER_STEP = 4
N_SPLIT = 3
AUG_W = 2 * N_SPLIT


def _dot(a, b):
    return jnp.dot(a, b, preferred_element_type=jnp.float32)


def _rms_norm(x, g):
    ms = jnp.mean(x * x, axis=-1, keepdims=True)
    return x * lax.rsqrt(ms + EPS) * g


def _split_bf16(x):
    parts = []
    for _ in range(N_SPLIT):
        t = x.astype(jnp.bfloat16).astype(jnp.float32)
        parts.append(t)
        x = x - t
    return parts


def _aug_base(pair, sub):
    return AUG_W * pair + (HEAD_DIM if sub == 0 else 0)


def _aug_layout():
    place = np.zeros((N_SPLIT, LANES, 2 * LANES), np.float32)
    const = np.zeros((1, 2 * LANES), np.float32)
    for h in range(N_HEADS):
        base = _aug_base(h // HEADS_PER_STEP, h % HEADS_PER_STEP)
        for j in range(N_SPLIT):
            place[j, h, base + j] = 1.0
            const[0, base + N_SPLIT + j] = 1.0
            const[0, LANES + base + j] = 1.0
            place[j, h, LANES + base + N_SPLIT + j] = -1.0
    return jnp.asarray(place, jnp.bfloat16), jnp.asarray(const)


def _layer(arr, l, cols=None, col_block=0):
    block = arr.shape[1:] if cols is None else arr.shape[1:-1] + (cols,)
    return arr, (None,) + block, (l,) + (0,) * (len(block) - 1) + (col_block,)


def _whole(arr):
    return arr, arr.shape, (0,) * arr.ndim


def _const_spec(operand):
    _, block, index = operand
    return pl.BlockSpec(block, lambda *_: index, pipeline_mode=pl.Buffered(1))


def _swiglu(h, wg_ref, wu_ref, wd_ref):
    acc = jnp.zeros((h.shape[0], D_MODEL), jnp.float32)
    for c in range(D_FF // FFN_FC):
        sl = slice(c * FFN_FC, (c + 1) * FFN_FC)
        gate = _dot(h, wg_ref[:, sl])
        up = _dot(h, wu_ref[:, sl])
        act = (gate * jax.nn.sigmoid(gate) * up).astype(jnp.bfloat16)
        acc = acc + _dot(act, wd_ref[sl, :])
    return acc


def _ffn_kernel(x_ref, g_ref, wg_ref, wu_ref, wd_ref, o_ref):
    x = x_ref[...]
    h = _rms_norm(x, g_ref[...]).astype(jnp.bfloat16)
    o_ref[...] = x + 0.5 * _swiglu(h, wg_ref, wu_ref, wd_ref)


def _ffn(x2d, g, wg, wu, wd):
    n = x2d.shape[0]
    row = pl.BlockSpec((FFN_TM, D_MODEL), lambda i: (i, 0))
    return pl.pallas_call(
        _ffn_kernel,
        grid=(n // FFN_TM,),
        in_specs=[row] + [_const_spec(c) for c in (g, wg, wu, wd)],
        out_specs=row,
        out_shape=jax.ShapeDtypeStruct(x2d.shape, jnp.float32),
        compiler_params=pltpu.CompilerParams(
            dimension_semantics=("parallel",), vmem_limit_bytes=VMEM_LIMIT),
        name="ffn",
    )(x2d, *(c[0] for c in (g, wg, wu, wd)))


def _linear_scan(a, b, h0):
    tm, c = a.shape
    groups = tm // SUBLANES
    a3 = a.reshape(groups, SUBLANES, c)
    b3 = b.reshape(groups, SUBLANES, c)
    r = lax.broadcasted_iota(jnp.int32, a3.shape, 1)
    d = 1
    while d < SUBLANES:
        keep = r >= d
        a_s = jnp.where(keep, pltpu.roll(a3, d, axis=1), 1.0)
        b_s = jnp.where(keep, pltpu.roll(b3, d, axis=1), 0.0)
        b3 = a3 * b_s + b3
        a3 = a3 * a_s
        d *= 2
    carry = h0
    out = []
    for i in range(groups):
        hg = a3[i] * carry + b3[i]
        out.append(hg)
        carry = hg[SUBLANES - 1:SUBLANES, :]
    return jnp.concatenate(out, axis=0), carry


def _head_rms_scale(x):
    sq = x * x
    low = lax.broadcasted_iota(jnp.int32, (x.shape[0], LANES), 1) < HEAD_DIM
    outs = []
    for c in range(x.shape[1] // LANES):
        t = sq[:, c * LANES:(c + 1) * LANES]
        s0 = jnp.sum(jnp.where(low, t, 0.0), axis=-1, keepdims=True)
        s1 = jnp.sum(jnp.where(low, 0.0, t), axis=-1, keepdims=True)
        r0 = lax.rsqrt(s0 * (1.0 / HEAD_DIM) + EPS)
        r1 = lax.rsqrt(s1 * (1.0 / HEAD_DIM) + EPS)
        outs.append(jnp.where(low, r0, r1))
    return jnp.concatenate(outs, axis=1)


def _mixer_in_kernel(x_ref, g_ref, wmain_ref, wf_ref, wgate_ref, bgate_ref,
                     convw_ref, convb_ref, wax_ref, ba_ref, bx_ref, lam_ref,
                     gq_ref, gk_ref, bf_ref, wlru_ref, place_ref, augc_ref,
                     q_ref, k_ref, v_ref, augq_ref, augk_ref, p_ref, ga_ref,
                     conv_tail, h_carry, cum_carry):
    tm = x_ref.shape[1]

    @pl.when(pl.program_id(1) == 0)
    def _():
        conv_tail[...] = jnp.zeros_like(conv_tail)
        h_carry[...] = jnp.zeros_like(h_carry)
        cum_carry[...] = jnp.zeros_like(cum_carry)

    x = x_ref[0]
    h = _rms_norm(x, g_ref[...]).astype(jnp.bfloat16)
    off = 2 * D_RNN

    z_lx = _dot(h, wmain_ref[:, 0:D_RNN])
    tail = conv_tail[...]
    rows8 = lax.broadcasted_iota(jnp.int32, tail.shape, 0)
    xr = z_lx * convw_ref[CONV_WIDTH - 1:CONV_WIDTH, :] + convb_ref[...]
    for j in range(1, CONV_WIDTH):
        rolled = pltpu.roll(z_lx, j, axis=0)
        head = jnp.where(rows8 < j, pltpu.roll(tail, j, axis=0), rolled[:SUBLANES])
        shifted = jnp.concatenate([head, rolled[SUBLANES:]], axis=0)
        xr = xr + shifted * convw_ref[CONV_WIDTH - 1 - j:CONV_WIDTH - j, :]
    conv_tail[...] = z_lx[tm - SUBLANES:, :]

    f = _dot(h, wf_ref[...]) + bf_ref[...]
    c = -(jnp.maximum(-f, 0.0) + jnp.log1p(jnp.exp(-jnp.abs(f))))
    c, c_last = _linear_scan(jnp.ones_like(c), c, cum_carry[...])
    cum_carry[...] = c_last
    aug = augc_ref[...]
    for j, part in enumerate(_split_bf16(c)):
        aug = aug + _dot(part.astype(jnp.bfloat16), place_ref[j])
    augq_ref[0] = aug[:, :LANES].astype(jnp.bfloat16)
    augk_ref[0] = aug[:, LANES:].astype(jnp.bfloat16)

    xr_b = xr.astype(jnp.bfloat16)
    r_parts, i_parts = [], []
    for n in range(N_RNN_BLOCKS):
        ri = _dot(xr_b[:, n * RNN_BLOCK:(n + 1) * RNN_BLOCK], wax_ref[n])
        r_parts.append(ri[:, :RNN_BLOCK])
        i_parts.append(ri[:, RNN_BLOCK:])
    q = _dot(h, wmain_ref[:, off:off + D_ATTN])
    r = jax.nn.sigmoid(jnp.concatenate(r_parts, axis=1) + ba_ref[...])
    i_gate = jax.nn.sigmoid(jnp.concatenate(i_parts, axis=1) + bx_ref[...])
    neg_lam = -lam_ref[...]
    softplus = jnp.maximum(neg_lam, 0.0) + jnp.log1p(jnp.exp(-jnp.abs(neg_lam)))
    log_a = -LRU_C * r * softplus
    a = jnp.exp(log_a)
    th = jnp.tanh(log_a)
    b = jnp.sqrt(-2.0 * th / (1.0 - th)) * (i_gate * xr)
    k = _dot(h, wmain_ref[:, off + D_ATTN:off + 2 * D_ATTN])
    hr, h_last = _linear_scan(a, b, h_carry[...])
    h_carry[...] = h_last
    v_ref[0] = _dot(h, wmain_ref[:, off + 2 * D_ATTN:off + 3 * D_ATTN]).astype(jnp.bfloat16)

    z_lg = _dot(h, wmain_ref[:, D_RNN:2 * D_RNN])
    y = (jax.nn.gelu(z_lg) * hr).astype(jnp.bfloat16)
    y_lru = _dot(y, wlru_ref[...])
    gates_pre = _dot(h, wgate_ref[...])
    q_ref[0] = (q * _head_rms_scale(q) * gq_ref[...]).astype(jnp.bfloat16)
    gates = jax.nn.sigmoid(gates_pre + bgate_ref[...])
    ga_ref[0] = gates[:, D_MODEL:]
    k_ref[0] = (k * _head_rms_scale(k) * gk_ref[...]).astype(jnp.bfloat16)
    p_ref[0] = gates[:, :D_MODEL] * y_lru


def _mixer_in(x, g, wmain, wf, wgate, bgate, convw, convb, wax, ba, bx, lam,
              gq, gk, bf, wlru, place, augc):
    bsz, seq, _ = x.shape
    tile = lambda w: pl.BlockSpec((1, MIX_TM, w), lambda b, t: (b, t, 0))
    consts = (g, wmain, wf, wgate, bgate, convw, convb, wax, ba, bx, lam, gq, gk, bf, wlru,
              place, augc)
    act = lambda w, dt: jax.ShapeDtypeStruct((bsz, seq, w), dt)
    return pl.pallas_call(
        _mixer_in_kernel,
        grid=(bsz, seq // MIX_TM),
        in_specs=[tile(D_MODEL)] + [_const_spec(c) for c in consts],
        out_specs=[tile(D_ATTN), tile(D_ATTN), tile(D_ATTN), tile(LANES), tile(LANES),
                   tile(D_MODEL), tile(D_MODEL)],
        out_shape=[act(D_ATTN, jnp.bfloat16), act(D_ATTN, jnp.bfloat16),
                   act(D_ATTN, jnp.bfloat16), act(LANES, jnp.bfloat16), act(LANES, jnp.bfloat16),
                   act(D_MODEL, jnp.float32), act(D_MODEL, jnp.float32)],
        scratch_shapes=[pltpu.VMEM((SUBLANES, D_RNN), jnp.float32),
                        pltpu.VMEM((1, D_RNN), jnp.float32),
                        pltpu.VMEM((1, LANES), jnp.float32)],
        compiler_params=pltpu.CompilerParams(
            dimension_semantics=("parallel", "arbitrary"), vmem_limit_bytes=VMEM_LIMIT),
        name="mixer_in",
    )(x, *(c[0] for c in consts))


def _attn_kernel(q_ref, k_ref, v_ref, aq_ref, ak_ref, o_ref):
    pair0 = pl.program_id(1) * PAIRS_PER_STEP
    qi = pl.program_id(2)
    t = ATT_T
    seq = k_ref.shape[1]
    heads = [(pr, h) for pr in range(PAIRS_PER_STEP) for h in range(HEADS_PER_STEP)]

    lane = lax.broadcasted_iota(jnp.int32, (1, LANES), 1)
    row_mask = lambda cond: jnp.where(cond, 1.0, 0.0).astype(jnp.bfloat16)
    lane_b = lax.broadcasted_iota(jnp.int32, (t, LANES), 1).astype(jnp.float32).astype(jnp.bfloat16)
    low = lane_b < HEAD_DIM
    own = [low, jnp.logical_not(low)]
    den = [jnp.broadcast_to(row_mask(lane == HEAD_DIM), (t, LANES)),
           jnp.broadcast_to(row_mask(lane == 0), (t, LANES))]
    pair_bias = []
    for pr in range(PAIRS_PER_STEP):
        b0, b1 = _aug_base(pair0 + pr, 0), _aug_base(pair0 + pr, 1)
        pair_bias.append(row_mask(((lane >= b0) & (lane < b0 + AUG_W))
                                  | ((lane >= b1) & (lane < b1 + AUG_W))))
    scale = jnp.bfloat16(HEAD_DIM ** -0.5)
    qa = []
    for pr in range(PAIRS_PER_STEP):
        aq = aq_ref[0] * pair_bias[pr]
        qs = q_ref[0, :, pr * LANES:(pr + 1) * LANES] * scale
        qa += [jnp.where(own[h], qs, aq) for h in range(HEADS_PER_STEP)]
    def qk(q_rows, keys, i, mask):
        pr, h = heads[i]
        ka = jnp.where(mask[h], k_ref[0, keys, pr * LANES:(pr + 1) * LANES],
                       ak_ref[0, keys, :] * pair_bias[pr])
        return lax.dot_general(q_rows, ka, (((1,), (1,)), ((), ())),
                               preferred_element_type=jnp.float32)

    def pv(p, keys, i, mask, den_rows):
        pr, h = heads[i]
        return _dot(p, jnp.where(mask[h], v_ref[0, keys, pr * LANES:(pr + 1) * LANES], den_rows[h]))

    def full_step(kj, carry):
        keys = pl.ds(kj * t, t)
        scores = [qk(qa[i], keys, i, own) for i in range(len(heads))]
        probs, maxes, alphas = [], [], []
        for i in range(len(heads)):
            m = carry[i][0]
            m_new = jnp.maximum(m, jnp.max(scores[i], axis=-1, keepdims=True))
            probs.append(jnp.exp(scores[i] - m_new).astype(jnp.bfloat16))
            alphas.append(jnp.exp(m - m_new))
            maxes.append(m_new)
        return tuple((maxes[i], alphas[i] * carry[i][1] + pv(probs[i], keys, i, own, den))
                     for i in range(len(heads)))

    half = t // 2
    lane_bh = lax.broadcasted_iota(jnp.int32, (half, LANES), 1).astype(jnp.float32).astype(jnp.bfloat16)
    low_h = lane_bh < HEAD_DIM
    own_h = [low_h, jnp.logical_not(low_h)]
    den_h = [d[:half] for d in den]
    tri = (lax.broadcasted_iota(jnp.int32, (half, half), 0)
           >= lax.broadcasted_iota(jnp.int32, (half, half), 1))

    def diag_step(kj, carry):
        keys_l, keys_r = pl.ds(kj * t, half), pl.ds(kj * t + half, half)
        scores = []
        for i in range(len(heads)):
            s_l = qk(qa[i], keys_l, i, own_h)
            s_r = qk(qa[i][half:], keys_r, i, own_h)
            scores.append((jnp.where(tri, s_l[:half], -jnp.inf), s_l[half:],
                           jnp.where(tri, s_r, -jnp.inf)))
        parts = []
        for i in range(len(heads)):
            m = carry[i][0]
            s_lt, s_lb, s_r = scores[i]
            m_top = jnp.maximum(m[:half], jnp.max(s_lt, axis=-1, keepdims=True))
            m_bot = jnp.maximum(m[half:], jnp.maximum(jnp.max(s_lb, axis=-1, keepdims=True),
                                                      jnp.max(s_r, axis=-1, keepdims=True)))
            parts.append((jnp.exp(s_lt - m_top).astype(jnp.bfloat16),
                          jnp.exp(s_lb - m_bot).astype(jnp.bfloat16),
                          jnp.exp(s_r - m_bot).astype(jnp.bfloat16),
                          jnp.exp(m[:half] - m_top), jnp.exp(m[half:] - m_bot)))
        out = []
        for i in range(len(heads)):
            p_lt, p_lb, p_r, a_top, a_bot = parts[i]
            acc = carry[i][1]
            out.append((a_top * acc[:half] + pv(p_lt, keys_l, i, own_h, den_h),
                        a_bot * acc[half:] + pv(p_lb, keys_l, i, own_h, den_h)
                        + pv(p_r, keys_r, i, own_h, den_h)))
        return tuple(out)

    lane_h = lax.broadcasted_iota(jnp.int32, (half, LANES), 1)
    for n_full in range(seq // t):
        @pl.when(qi == n_full)
        def _(n_full=n_full):
            carry = tuple((jnp.full((t, 1), -jnp.inf, jnp.float32),
                           jnp.zeros((t, LANES), jnp.float32)) for _ in heads)
            for kj in range(n_full):
                carry = full_step(kj, carry)
            halves = diag_step(n_full, carry)
            for pr in range(PAIRS_PER_STEP):
                for r, rows_o in enumerate((slice(0, half), slice(half, t))):
                    a0, a1 = halves[HEADS_PER_STEP * pr][r], halves[HEADS_PER_STEP * pr + 1][r]
                    o0 = a0 / a0[:, HEAD_DIM:HEAD_DIM + 1]
                    o1 = a1 / a1[:, 0:1]
                    o_ref[0, rows_o, pr * LANES:(pr + 1) * LANES] = (
                        jnp.where(lane_h < HEAD_DIM, o0, o1).astype(jnp.bfloat16))


def _attention(q, k, v, augq, augk):
    bsz, seq, _ = q.shape
    n_pairs = N_HEADS // HEADS_PER_STEP
    width = PAIRS_PER_STEP * LANES
    qspec = pl.BlockSpec((1, ATT_T, width), lambda b, p, i: (b, i, p))
    kvspec = pl.BlockSpec((1, seq, width), lambda b, p, i: (b, 0, p))
    return pl.pallas_call(
        _attn_kernel,
        grid=(bsz, n_pairs // PAIRS_PER_STEP, seq // ATT_T),
        in_specs=[qspec, kvspec, kvspec,
                  pl.BlockSpec((1, ATT_T, LANES), lambda b, p, i: (b, i, 0)),
                  pl.BlockSpec((1, seq, LANES), lambda b, p, i: (b, 0, 0))],
        out_specs=qspec,
        out_shape=jax.ShapeDtypeStruct(q.shape, jnp.bfloat16),
        compiler_params=pltpu.CompilerParams(
            dimension_semantics=("parallel", "parallel", "parallel"),
            vmem_limit_bytes=VMEM_LIMIT),
        name="attention",
    )(q, k, v, augq, augk)


def _mixer_out_kernel(x_ref, o_ref, p_ref, ga_ref, wattn_ref, wo_ref,
                      g_ref, wg_ref, wu_ref, wd_ref, out_ref):
    y_attn = _dot(o_ref[...], wattn_ref[...])
    m = (p_ref[...] + ga_ref[...] * y_attn).astype(jnp.bfloat16)
    x = x_ref[...] + _dot(m, wo_ref[...])
    h = _rms_norm(x, g_ref[...]).astype(jnp.bfloat16)
    out_ref[...] = x + 0.5 * _swiglu(h, wg_ref, wu_ref, wd_ref)


def _mixer_out(x2d, o2d, p2d, ga2d, wattn, wo, g, wg, wu, wd):
    n = x2d.shape[0]
    row = pl.BlockSpec((FFN_TM, D_MODEL), lambda i: (i, 0))
    consts = (wattn, wo, g, wg, wu, wd)
    return pl.pallas_call(
        _mixer_out_kernel,
        grid=(n // FFN_TM,),
        in_specs=[row, row, row, row] + [_const_spec(c) for c in consts],
        out_specs=row,
        out_shape=jax.ShapeDtypeStruct(x2d.shape, jnp.float32),
        compiler_params=pltpu.CompilerParams(
            dimension_semantics=("parallel",), vmem_limit_bytes=VMEM_LIMIT),
        name="mixer_out",
    )(x2d, o2d, p2d, ga2d, *(c[0] for c in consts))


def kernel(x, g_ffn1, w_up1, w_down1, g_mix, w_in, b_gate, conv_w, conv_b, w_a, b_a, w_x, b_x, lam, g_q, g_k, b_forget, w_lru_out, w_attn_out, w_o, g_ffn2, w_up2, w_down2):
    bsz, seq, _ = x.shape
    depth = g_ffn1.shape[0]
    bf16 = lambda w: w.astype(jnp.bfloat16)
    vec = lambda p: p.reshape(p.shape[0], 1, -1)
    wup1, wdn1, wup2, wdn2 = bf16(w_up1), bf16(w_down1), bf16(w_up2), bf16(w_down2)
    win = bf16(w_in)
    wf = bf16(jnp.pad(w_in[:, :, OFF_F:OFF_GATE], ((0, 0), (0, 0), (0, LANES - N_HEADS))))
    wgate = bf16(w_in[:, :, OFF_GATE:])
    bfp = vec(jnp.pad(b_forget, ((0, 0), (0, LANES - N_HEADS))))
    wax = bf16(jnp.concatenate([w_a, w_x], axis=-1))
    gq, gk = vec(jnp.tile(g_q, (1, N_HEADS))), vec(jnp.tile(g_k, (1, N_HEADS)))
    wlru, wattn, wo = bf16(w_lru_out), bf16(w_attn_out), bf16(w_o)
    place, augc = _aug_layout()

    x2d = x.reshape(bsz * seq, D_MODEL)
    for l in range(depth):
        x2d = _ffn(x2d, _layer(vec(g_ffn1), l), _layer(wup1, l, D_FF, 0), _layer(wup1, l, D_FF, 1),
                   _layer(wdn1, l))

        q, k, v, augq, augk, p, ga = _mixer_in(
            x2d.reshape(bsz, seq, D_MODEL), _layer(vec(g_mix), l), _layer(win, l, OFF_F, 0),
            _layer(wf, l), _layer(wgate, l), _layer(vec(b_gate), l), _layer(conv_w, l),
            _layer(vec(conv_b), l), _layer(wax, l), _layer(vec(b_a), l), _layer(vec(b_x), l),
            _layer(vec(lam), l), _layer(gq, l), _layer(gk, l), _layer(bfp, l), _layer(wlru, l),
            _whole(place), _whole(augc))

        o = _attention(q, k, v, augq, augk)

        flat = lambda a: a.reshape(bsz * seq, -1)
        x2d = _mixer_out(x2d, flat(o), flat(p), flat(ga), _layer(wattn, l), _layer(wo, l),
                         _layer(vec(g_ffn2), l), _layer(wup2, l, D_FF, 0), _layer(wup2, l, D_FF, 1),
                         _layer(wdn2, l))
    return x2d.reshape(bsz, seq, D_MODEL)
```
